```python
import jax, jax.numpy as jnp
from jax import lax
import numpy as np

D_MODEL = 2048
BATCH = 4
SEQ = 4096
DEPTH = 2

BRANCH_W = D_MODEL // 2
N_BRANCH = 3
RW_HEAD = 64
RW_HEADS = BRANCH_W // RW_HEAD
RW_DECAY_RANK = 64
RW_AAA_RANK = 64
RW_LN_EPS = 64e-5
RW_COLS = 4 * BRANCH_W + RW_DECAY_RANK + RW_AAA_RANK
SSD_HEAD = 64
SSD_HEADS = BRANCH_W // SSD_HEAD
SSD_GROUPS = 4
SSD_STATE = 128
SSD_CONV = 4
SSD_CHUNK = 128
SSD_XBC = BRANCH_W + 2 * SSD_GROUPS * SSD_STATE
SSD_COLS = BRANCH_W + SSD_XBC + SSD_HEADS
SSD_NORM_EPS = 1e-5
SC_CONV = 3
SC_COLS = 4 * BRANCH_W
GATE_COLS = N_BRANCH * D_MODEL
IN_COLS = RW_COLS + SSD_COLS + SC_COLS + GATE_COLS
NORM_EPS = 1e-6

kernel_name = "hybrid_rwkv7_mamba2_shortconv_gated_merge"


def rmsnorm(x, w, eps=NORM_EPS):
    x32 = x.astype(jnp.float32)
    y = x32 * lax.rsqrt(jnp.mean(x32 * x32, axis=-1, keepdims=True) + eps)
    return (y * w.astype(jnp.float32)).astype(x.dtype)


def causal_dwconv(x, w):
    k_width, ch = w.shape
    return lax.conv_general_dilated(
        x, w[:, None, :].astype(x.dtype), window_strides=(1,),
        padding=[(k_width - 1, 0)], dimension_numbers=('NWC', 'WIO', 'NWC'),
        feature_group_count=ch)


def token_shift(p, mu):
    prev = jnp.pad(p, ((0, 0), (1, 0), (0, 0)))[:, :-1]
    return p + mu * (prev - p)


def rwkv7_branch(pa, mu, w0, w2, a0, a2, k_k, k_a, r_k, ln_w, ln_b):
    bsz, seq, _ = pa.shape
    pa = token_shift(pa.astype(jnp.float32), mu)
    r, k, v, g, dw, da = jnp.split(
        pa, [BRANCH_W, 2 * BRANCH_W, 3 * BRANCH_W, 4 * BRANCH_W, 4 * BRANCH_W + RW_DECAY_RANK], axis=-1)
    log_w = -jax.nn.softplus(-(w0 + jnp.tanh(dw) @ w2)) - 0.5
    decay = jnp.exp(-jnp.exp(log_w))
    a = jax.nn.sigmoid(a0 + da @ a2)
    kk = k * k_k
    k = k * (1.0 + (a - 1.0) * k_a)
    heads = lambda t: t.reshape(bsz, seq, RW_HEADS, RW_HEAD)
    r, k, v, kk, a, decay = (heads(t) for t in (r, k, v, kk, a, decay))
    kh = kk / jnp.maximum(jnp.sqrt(jnp.sum(kk * kk, axis=-1, keepdims=True)), 1e-12)
    bvec = kh * a

    def step(s, inp):
        r_t, w_t, k_t, v_t, kh_t, b_t = inp
        sa = jnp.einsum('bhvk,bhk->bhv', s, kh_t)
        s = s * w_t[:, :, None, :] - sa[..., None] * b_t[:, :, None, :] + v_t[..., None] * k_t[:, :, None, :]
        return s, jnp.einsum('bhvk,bhk->bhv', s, r_t)

    tm = lambda t: jnp.moveaxis(t, 1, 0)
    s0 = jnp.zeros((bsz, RW_HEADS, RW_HEAD, RW_HEAD), jnp.float32)
    _, y = lax.scan(step, s0, (tm(r), tm(decay), tm(k), tm(v), tm(kh), tm(bvec)))
    y = jnp.moveaxis(y, 0, 1)
    mean = jnp.mean(y, axis=-1, keepdims=True)
    var = jnp.mean(jnp.square(y - mean), axis=-1, keepdims=True)
    yn = (y - mean) * lax.rsqrt(var + RW_LN_EPS)
    yn = yn * ln_w.reshape(RW_HEADS, RW_HEAD) + ln_b.reshape(RW_HEADS, RW_HEAD)
    bonus = jnp.sum(r * k * r_k, axis=-1, keepdims=True) * v
    return (yn + bonus).reshape(bsz, seq, BRANCH_W) * jax.nn.silu(g)


def ssd_chunked(xdt, da, bm, cm):
    bsz, seq, n_heads, hd = xdt.shape
    n_chunks = seq // SSD_CHUNK
    hpg = n_heads // SSD_GROUPS
    xc = xdt.reshape(bsz, n_chunks, SSD_CHUNK, SSD_GROUPS, hpg, hd)
    bc = bm.reshape(bsz, n_chunks, SSD_CHUNK, SSD_GROUPS, SSD_STATE)
    cc = cm.reshape(bsz, n_chunks, SSD_CHUNK, SSD_GROUPS, SSD_STATE)
    a_cs = jnp.cumsum(da.reshape(bsz, n_chunks, SSD_CHUNK, SSD_GROUPS, hpg).transpose(0, 1, 3, 4, 2), axis=-1)
    causal = jnp.tril(jnp.ones((SSD_CHUNK, SSD_CHUNK), dtype=bool))
    seg = jnp.exp(jnp.where(causal, a_cs[..., :, None] - a_cs[..., None, :], -jnp.inf))
    scores = jnp.einsum('bclgn,bcsgn->bcgls', cc, bc)
    y_diag = jnp.einsum('bcgjls,bcsgjp->bclgjp', scores[:, :, :, None] * seg, xc)
    decay_to_end = jnp.exp(a_cs[..., -1:] - a_cs)
    chunk_states = jnp.einsum('bclgn,bcgjl,bclgjp->bcgjpn', bc, decay_to_end, xc)
    chunk_decay = jnp.exp(a_cs[..., -1])

    def carry_state(s, inp):
        st, dec = inp
        return s * dec[..., None, None] + st, s

    s0 = jnp.zeros((bsz, SSD_GROUPS, hpg, hd, SSD_STATE), xc.dtype)
    _, s_in = lax.scan(carry_state, s0, (jnp.moveaxis(chunk_states, 1, 0), jnp.moveaxis(chunk_decay, 1, 0)))
    s_in = jnp.moveaxis(s_in, 0, 1)
    y_off = jnp.einsum('bclgn,bcgjpn,bcgjl->bclgjp', cc, s_in, jnp.exp(a_cs))
    return (y_diag + y_off).reshape(bsz, seq, n_heads, hd)


def mamba2_branch(pb, conv_w, conv_b, dt_bias, a_log, d_skip, norm_w):
    bsz, seq, _ = pb.shape
    z, xbc, dt = jnp.split(pb, [BRANCH_W, BRANCH_W + SSD_XBC], axis=-1)
    xbc = jax.nn.silu(causal_dwconv(xbc, conv_w) + conv_b)
    xs, bm, cm = jnp.split(xbc, [BRANCH_W, BRANCH_W + SSD_GROUPS * SSD_STATE], axis=-1)
    xs = xs.reshape(bsz, seq, SSD_HEADS, SSD_HEAD).astype(jnp.float32)
    bm = bm.reshape(bsz, seq, SSD_GROUPS, SSD_STATE).astype(jnp.float32)
    cm = cm.reshape(bsz, seq, SSD_GROUPS, SSD_STATE).astype(jnp.float32)
    dt = jax.nn.softplus(dt.astype(jnp.float32) + dt_bias)
    a = -jnp.exp(a_log.astype(jnp.float32))
    y = ssd_chunked(xs * dt[..., None], dt * a, bm, cm)
    y = y + d_skip[:, None] * xs
    y = y.reshape(bsz, seq, BRANCH_W) * jax.nn.silu(z.astype(jnp.float32))
    y = rmsnorm(y.reshape(bsz, seq, SSD_GROUPS, BRANCH_W // SSD_GROUPS),
                norm_w.reshape(SSD_GROUPS, BRANCH_W // SSD_GROUPS), SSD_NORM_EPS)
    return y.reshape(bsz, seq, BRANCH_W)


def short_conv_branch(pc, conv_w):
    bg, cg, xin, g = jnp.split(pc, 4, axis=-1)
    return bg * causal_dwconv(cg * xin, conv_w) * jax.nn.silu(g)


def setup_inputs(seed: int = 0) -> dict:
    key = jax.random.key(seed)
    ks = jax.random.split(key, 24)
    f32 = jnp.float32
    nrm = lambda k, shape, s: jax.random.normal(k, shape, f32) * s
    dt0 = jnp.exp(jax.random.uniform(ks[14], (DEPTH, SSD_HEADS), f32) * (np.log(0.1) - np.log(0.001)) + np.log(0.001))
    return {
        'x': nrm(ks[0], (BATCH, SEQ, D_MODEL), 1.0),
        'norm_w': 1.0 + nrm(ks[1], (DEPTH, D_MODEL), 0.05),
        'w_in': nrm(ks[2], (DEPTH, D_MODEL, IN_COLS), D_MODEL ** -0.5),
        'rw_mu': jax.random.uniform(ks[3], (DEPTH, RW_COLS), f32),
        'rw_w0': jax.random.uniform(ks[4], (DEPTH, BRANCH_W), f32, -6.0, 1.0),
        'rw_w2': nrm(ks[5], (DEPTH, RW_DECAY_RANK, BRANCH_W), 0.5 * RW_DECAY_RANK ** -0.5),
        'rw_a0': nrm(ks[6], (DEPTH, BRANCH_W), 0.5),
        'rw_a2': nrm(ks[7], (DEPTH, RW_AAA_RANK, BRANCH_W), 0.5 * RW_AAA_RANK ** -0.5),
        'rw_kk': 0.85 + nrm(ks[8], (DEPTH, BRANCH_W), 0.05),
        'rw_ka': 1.0 + nrm(ks[9], (DEPTH, BRANCH_W), 0.05),
        'rw_rk': nrm(ks[10], (DEPTH, RW_HEADS, RW_HEAD), 0.1),
        'rw_ln_w': 1.0 + nrm(ks[11], (DEPTH, BRANCH_W), 0.05),
        'rw_ln_b': nrm(ks[12], (DEPTH, BRANCH_W), 0.01),
        'ssd_conv_w': nrm(ks[13], (DEPTH, SSD_CONV, SSD_XBC), SSD_CONV ** -0.5),
        'ssd_conv_b': nrm(ks[15], (DEPTH, SSD_XBC), 0.01),
        'ssd_dt_bias': dt0 + jnp.log(-jnp.expm1(-dt0)),
        'ssd_a_log': jnp.log(jax.random.uniform(ks[16], (DEPTH, SSD_HEADS), f32, 1.0, 16.0)),
        'ssd_d': 1.0 + nrm(ks[17], (DEPTH, SSD_HEADS), 0.1),
        'ssd_norm_w': 1.0 + nrm(ks[18], (DEPTH, BRANCH_W), 0.05),
        'sc_conv_w': nrm(ks[19], (DEPTH, SC_CONV, BRANCH_W), SC_CONV ** -0.5),
        'w_branch': nrm(ks[20], (DEPTH, N_BRANCH, BRANCH_W, D_MODEL), BRANCH_W ** -0.5),
        'w_out': nrm(ks[21], (DEPTH, D_MODEL, D_MODEL), D_MODEL ** -0.5),
        'final_norm_w': 1.0 + nrm(ks[22], (D_MODEL,), 0.05),
    }


def reference(x, norm_w, w_in, rw_mu, rw_w0, rw_w2, rw_a0, rw_a2, rw_kk, rw_ka, rw_rk,
              rw_ln_w, rw_ln_b, ssd_conv_w, ssd_conv_b, ssd_dt_bias, ssd_a_log, ssd_d,
              ssd_norm_w, sc_conv_w, w_branch, w_out, final_norm_w):
    bsz, seq, _ = x.shape
    for l in range(DEPTH):
        h = rmsnorm(x, norm_w[l])
        p = h @ w_in[l]
        pa, pb, pc, pg = jnp.split(
            p, [RW_COLS, RW_COLS + SSD_COLS, RW_COLS + SSD_COLS + SC_COLS], axis=-1)
        oa = rwkv7_branch(pa, rw_mu[l], rw_w0[l], rw_w2[l], rw_a0[l], rw_a2[l], rw_kk[l],
                          rw_ka[l], rw_rk[l], rw_ln_w[l], rw_ln_b[l]).astype(x.dtype)
        ob = mamba2_branch(pb, ssd_conv_w[l], ssd_conv_b[l], ssd_dt_bias[l], ssd_a_log[l],
                           ssd_d[l], ssd_norm_w[l]).astype(x.dtype)
        oc = short_conv_branch(pc, sc_conv_w[l]).astype(x.dtype)
        o = jnp.stack([oa, ob, oc], axis=2)
        yb = jnp.einsum('btnw,nwd->btnd', o, w_branch[l])
        gates = jax.nn.sigmoid(pg.reshape(bsz, seq, N_BRANCH, D_MODEL))
        merged = jnp.sum(gates * yb, axis=2)
        x = x + merged @ w_out[l]
    return rmsnorm(x, final_norm_w)
```

```python
import functools
import math

import jax
import jax.numpy as jnp
from jax import lax
from jax.experimental import pallas as pl
from jax.experimental.pallas import tpu as pltpu

F32 = jnp.float32
BF16 = jnp.bfloat16

D_MODEL = 2048
DEPTH = 2
BRANCH_W = D_MODEL // 2
N_BRANCH = 3
RW_HEAD = 64
RW_HEADS = BRANCH_W // RW_HEAD
RW_RANK = 64
RW_LN_EPS = 64e-5
RW_COLS = 4 * BRANCH_W + 2 * RW_RANK
SSD_HEAD = 64
SSD_HEADS = BRANCH_W // SSD_HEAD
SSD_GROUPS = 4
SSD_STATE = 128
SSD_CONV = 4
SSD_XBC = BRANCH_W + 2 * SSD_GROUPS * SSD_STATE
SSD_COLS = BRANCH_W + SSD_XBC + SSD_HEADS
SSD_NORM_EPS = 1e-5
SC_CONV = 3
SC_COLS = 4 * BRANCH_W
GATE_COLS = N_BRANCH * D_MODEL
NORM_EPS = 1e-6

LANES = 128
SSD_COLS_PAD = 3200
RW_CHUNK = 64
SSD_CHUNK = 128
HALO = 8
VMEM_LIMIT = 48 * 1024 * 1024


def _cparams(sem):
    return pltpu.CompilerParams(dimension_semantics=sem, vmem_limit_bytes=VMEM_LIMIT)


def _sigmoid(x):
    return 1.0 / (1.0 + jnp.exp(-x))


def _silu(x):
    return x * _sigmoid(x)


def _softplus(x):
    return jnp.maximum(x, 0.0) + jnp.log1p(jnp.exp(-jnp.abs(x)))


def _dot(a, b):
    return jnp.dot(a, b, preferred_element_type=F32)


def _dot_nt(a, b):
    return lax.dot_general(a, b, (((1,), (1,)), ((), ())), preferred_element_type=F32)


def _dot_tn(a, b):
    return lax.dot_general(a, b, (((0,), (0,)), ((), ())), preferred_element_type=F32)


def _split3(x):
    hi = x.astype(BF16)
    r1 = x - hi.astype(F32)
    mid = r1.astype(BF16)
    lo = (r1 - mid.astype(F32)).astype(BF16)
    return hi, mid, lo


def _dot3(x, m):
    hi, mid, lo = _split3(x)
    return _dot(hi, m) + _dot(mid, m) + _dot(lo, m)


def _dot3_left(m, x):
    hi, mid, lo = _split3(x)
    return _dot(m, hi) + _dot(m, mid) + _dot(m, lo)


def _rmsnorm_kernel(x_ref, w_ref, o_ref):
    x = x_ref[...]
    ms = jnp.mean(x * x, axis=-1, keepdims=True)
    o_ref[...] = (x * lax.rsqrt(ms + NORM_EPS) * w_ref[...]).astype(o_ref.dtype)


def _rmsnorm(x, w, out_dtype, tm=512):
    m, d = x.shape
    return pl.pallas_call(
        _rmsnorm_kernel,
        grid=(m // tm,),
        in_specs=[pl.BlockSpec((tm, d), lambda i: (i, 0)),
                  pl.BlockSpec((1, d), lambda i: (0, 0))],
        out_specs=pl.BlockSpec((tm, d), lambda i: (i, 0)),
        out_shape=jax.ShapeDtypeStruct((m, d), out_dtype),
        compiler_params=_cparams(("parallel",)),
    )(x, w.reshape(1, d))


def _matmul_kernel(a_ref, b_ref, o_ref):
    o_ref[...] = _dot(a_ref[...], b_ref[...]).astype(o_ref.dtype)


def _matmul(a, b, out_dtype, tm, tn):
    m, k = a.shape
    _, n = b.shape
    return pl.pallas_call(
        _matmul_kernel,
        grid=(m // tm, n // tn),
        in_specs=[pl.BlockSpec((tm, k), lambda i, j: (i, 0)),
                  pl.BlockSpec((k, tn), lambda i, j: (0, j))],
        out_specs=pl.BlockSpec((tm, tn), lambda i, j: (i, j)),
        out_shape=jax.ShapeDtypeStruct((m, n), out_dtype),
        compiler_params=_cparams(("parallel", "parallel")),
    )(a, b)


def _matmul_residual_kernel(a_ref, b_ref, x_ref, o_ref):
    o_ref[...] = x_ref[...] + _dot(a_ref[...], b_ref[...])


def _matmul_residual(a, b, x, tm, tn):
    m, k = a.shape
    _, n = b.shape
    return pl.pallas_call(
        _matmul_residual_kernel,
        grid=(m // tm, n // tn),
        in_specs=[pl.BlockSpec((tm, k), lambda i, j: (i, 0)),
                  pl.BlockSpec((k, tn), lambda i, j: (0, j)),
                  pl.BlockSpec((tm, tn), lambda i, j: (i, j))],
        out_specs=pl.BlockSpec((tm, tn), lambda i, j: (i, j)),
        out_shape=jax.ShapeDtypeStruct((m, n), F32),
        compiler_params=_cparams(("parallel", "parallel")),
    )(a, b, x)


def _rwkv_kernel(pa_ref, mu_ref, w0_ref, w2a2_ref, a0_ref, kk_ref, ka_ref, rk_ref,
                 lnw_ref, lnb_ref, o_ref, prev_ref, s_ref):
    c = RW_CHUNK
    n = RW_HEAD

    @pl.when(pl.program_id(1) == 0)
    def _():
        prev_ref[...] = jnp.zeros_like(prev_ref)
        s_ref[...] = jnp.zeros_like(s_ref)

    p = pa_ref[0]
    row = lax.broadcasted_iota(jnp.int32, p.shape, 0)
    shifted = jnp.where(row == 0, prev_ref[...], pltpu.roll(p, 1, axis=0))
    prev_ref[...] = p[c - 1:c, :]
    ps = p + mu_ref[...] * (shifted - p)

    r = ps[:, 0:BRANCH_W]
    k = ps[:, BRANCH_W:2 * BRANCH_W]
    v = ps[:, 2 * BRANCH_W:3 * BRANCH_W]
    gate = _silu(ps[:, 3 * BRANCH_W:4 * BRANCH_W])
    z = ps[:, 4 * BRANCH_W:4 * BRANCH_W + 2 * RW_RANK]
    zlane = lax.broadcasted_iota(jnp.int32, z.shape, 1)
    zz = jnp.where(zlane < RW_RANK, jnp.tanh(z), z).astype(BF16)
    proj = _dot(zz, w2a2_ref[...])
    ld = -_sigmoid(w0_ref[...] + proj[:, :BRANCH_W]) * math.exp(-0.5)
    a = _sigmoid(a0_ref[...] + proj[:, BRANCH_W:])
    kkv = k * kk_ref[...]
    k2 = k * (1.0 + (a - 1.0) * ka_ref[...])
    rkr = r * k2 * rk_ref[...]

    ci = lax.broadcasted_iota(jnp.int32, (c, c), 0)
    cj = lax.broadcasted_iota(jnp.int32, (c, c), 1)
    strict = ci > cj
    incl = ci >= cj
    tril = jnp.where(incl, 1.0, 0.0).astype(BF16)
    eye = jnp.where(ci == cj, 1.0, 0.0)
    cum = _dot3_left(tril, ld)
    tot = cum[c - 1:c, :]
    e_cum = jnp.exp(cum)
    e_ncum = jnp.exp(-cum)
    e_cumx = jnp.exp(cum - ld)
    e_tot = jnp.exp(tot - cum)
    g_c = jnp.exp(tot)

    lvl_masks = []
    shift = 0
    while (1 << shift) < c:
        same_big = (ci >> (shift + 1)) == (cj >> (shift + 1))
        diff_small = (ci >> shift) != (cj >> shift)
        lvl_masks.append(jnp.logical_and(same_big, diff_small))
        shift += 1

    for h in range(RW_HEADS):
        sl = slice(h * n, (h + 1) * n)
        kkv_h = kkv[:, sl]
        ss = jnp.sum(kkv_h * kkv_h, axis=-1, keepdims=True)
        kh = kkv_h * (1.0 / jnp.maximum(jnp.sqrt(ss), 1e-12))
        b_h = kh * a[:, sl]
        k2_h = k2[:, sl]
        v_h = v[:, sl]
        kt = (kh * e_cumx[:, sl]).astype(BF16)
        bt = (b_h * e_ncum[:, sl]).astype(BF16)
        kkt = (k2_h * e_ncum[:, sl]).astype(BF16)
        rt = r[:, sl] * e_cum[:, sl]
        b_e = (b_h * e_tot[:, sl]).astype(BF16)
        k_e = (k2_h * e_tot[:, sl]).astype(BF16)
        v_b = v_h.astype(BF16)

        x = jnp.concatenate([kt, rt.astype(BF16)], axis=0)
        gb = _dot_nt(x, bt)
        gk = _dot_nt(x, kkt)
        l_m = jnp.where(strict, gb[:c], 0.0)
        arb = jnp.where(incl, gb[c:], 0.0).astype(BF16)
        akk = jnp.where(strict, gk[:c], 0.0).astype(BF16)
        ark = jnp.where(incl, gk[c:], 0.0).astype(BF16)

        minv = eye - jnp.where(lvl_masks[0], l_m, 0.0)
        for m in lvl_masks[1:]:
            coff = jnp.where(m, l_m, 0.0).astype(BF16)
            mb = minv.astype(BF16)
            minv = minv - _dot(_dot(mb, coff).astype(BF16), mb)
        mb = minv.astype(BF16)

        av = _dot(akk, v_b)
        w_m = _dot(mb, kt).astype(BF16)
        u_m = _dot(mb, av.astype(BF16)).astype(BF16)
        t_m = -_dot_tn(w_m, b_e)
        g_m = _dot_tn(v_b, k_e) - _dot_tn(u_m, b_e)
        q_m = rt - _dot(arb, w_m)
        y_i = _dot(ark, v_b) - _dot(arb, u_m)

        s = s_ref[h]
        s_b = s.astype(BF16)
        y = _dot_nt(q_m.astype(BF16), s_b) + y_i
        s_ref[h] = s * g_c[:, sl] + _dot(s_b, t_m.astype(BF16)) + g_m

        mean = jnp.mean(y, axis=-1, keepdims=True)
        yc = y - mean
        var = jnp.mean(yc * yc, axis=-1, keepdims=True)
        yn = yc * lax.rsqrt(var + RW_LN_EPS) * lnw_ref[:, sl] + lnb_ref[:, sl]
        bonus = jnp.sum(rkr[:, sl], axis=-1, keepdims=True) * v_h
        o_ref[0, :, sl] = (yn + bonus) * gate[:, sl]


def _rwkv_branch(pa, mu, w0, w2a2, a0, kk, ka, rk, lnw, lnb):
    bsz, seq, cols = pa.shape
    row = lambda t: t.reshape(1, -1)
    vec = lambda width: pl.BlockSpec((1, width), lambda b, c: (0, 0))
    return pl.pallas_call(
        _rwkv_kernel,
        grid=(bsz, seq // RW_CHUNK),
        in_specs=[pl.BlockSpec((1, RW_CHUNK, cols), lambda b, c: (b, c, 0)),
                  vec(cols), vec(BRANCH_W),
                  pl.BlockSpec((2 * RW_RANK, 2 * BRANCH_W), lambda b, c: (0, 0)),
                  vec(BRANCH_W), vec(BRANCH_W), vec(BRANCH_W), vec(BRANCH_W),
                  vec(BRANCH_W), vec(BRANCH_W)],
        out_specs=pl.BlockSpec((1, RW_CHUNK, BRANCH_W), lambda b, c: (b, c, 0)),
        out_shape=jax.ShapeDtypeStruct((bsz, seq, BRANCH_W), F32),
        scratch_shapes=[pltpu.VMEM((1, cols), F32),
                        pltpu.VMEM((RW_HEADS, RW_HEAD, RW_HEAD), F32)],
        compiler_params=_cparams(("parallel", "arbitrary")),
    )(pa, row(mu), row(w0), w2a2, row(a0), row(kk), row(ka), row(rk), row(lnw), row(lnb))


def _ssd_kernel(pb_ref, cw_ref, cb_ref, dtb_ref, alog_ref, d_ref, nw_ref, e64_ref, e128_ref,
                o_ref, ext_ref, st_ref):
    l = SSD_CHUNK
    gw = BRANCH_W // SSD_GROUPS
    hpg = SSD_HEADS // SSD_GROUPS

    @pl.when(pl.program_id(1) == 0)
    def _():
        ext_ref[0:HALO, :] = jnp.zeros((HALO, SSD_XBC), F32)
        st_ref[...] = jnp.zeros_like(st_ref)

    pb = pb_ref[0]
    zgate = _silu(pb[:, 0:BRANCH_W])
    xbc = pb[:, BRANCH_W:BRANCH_W + SSD_XBC]
    dts = pb[:, BRANCH_W + SSD_XBC:BRANCH_W + SSD_XBC + LANES]

    ext_ref[HALO:HALO + l, :] = xbc
    conv = cb_ref[...] + cw_ref[SSD_CONV - 1:SSD_CONV, :] * xbc
    for kk in range(SSD_CONV - 1):
        back = SSD_CONV - 1 - kk
        conv = conv + cw_ref[kk:kk + 1, :] * ext_ref[pl.ds(HALO - back, l), :]
    ext_ref[0:HALO, :] = xbc[l - HALO:l, :]
    xa = _silu(conv)
    xs = xa[:, 0:BRANCH_W]
    bm = xa[:, BRANCH_W:BRANCH_W + SSD_GROUPS * SSD_STATE]
    cm = xa[:, BRANCH_W + SSD_GROUPS * SSD_STATE:]

    dt16 = _softplus(dts + dtb_ref[...])
    da16 = dt16 * (-jnp.exp(alog_ref[...]))
    ri = lax.broadcasted_iota(jnp.int32, (l, l), 0)
    rj = lax.broadcasted_iota(jnp.int32, (l, l), 1)
    causal = ri >= rj
    tril = jnp.where(causal, 1.0, 0.0).astype(BF16)
    acs16 = _dot3_left(tril, da16)
    acs_t = acs16.T
    e64 = e64_ref[...]
    dt_full = _dot3(dt16, e64)
    acs_full = _dot3(acs16, e64)
    acs_b = _dot3(acs16, e128_ref[...])
    tot_full = acs_full[l - 1:l, :]
    dte = jnp.exp(tot_full - acs_full)
    eacs = jnp.exp(acs_full)
    dec = jnp.exp(tot_full)
    xdt = xs * dt_full
    xdte = (xdt * dte).astype(BF16)
    glane = lax.broadcasted_iota(jnp.int32, (l, gw), 1) // SSD_HEAD

    for g in range(SSD_GROUPS):
        gs = slice(g * gw, (g + 1) * gw)
        c_g = cm[:, g * SSD_STATE:(g + 1) * SSD_STATE].astype(BF16)
        b_g = bm[:, g * SSD_STATE:(g + 1) * SSD_STATE].astype(BF16)
        scores = _dot_nt(c_g, b_g)
        xdt_g = xdt[:, gs]
        p_parts = []
        x_parts = []
        for jj in range(hpg):
            j = g * hpg + jj
            diff = acs_b[:, j * LANES:(j + 1) * LANES] - acs_t[j:j + 1, :]
            seg = jnp.exp(jnp.where(causal, diff, -jnp.inf))
            p_parts.append((scores * seg).astype(BF16))
            x_parts.append(jnp.where(glane == jj, xdt_g, 0.0).astype(BF16))
        y_diag = _dot(jnp.concatenate(p_parts, axis=1), jnp.concatenate(x_parts, axis=0))
        st = st_ref[g]
        y_off = _dot(c_g, st.astype(BF16)) * eacs[:, gs]
        st_ref[g] = st * dec[:, gs] + _dot_tn(b_g, xdte[:, gs])
        y = (y_diag + y_off + d_ref[:, gs] * xs[:, gs]) * zgate[:, gs]
        ms = jnp.mean(y * y, axis=-1, keepdims=True)
        o_ref[0, :, gs] = (y * lax.rsqrt(ms + SSD_NORM_EPS) * nw_ref[:, gs]).astype(o_ref.dtype)


def _ssd_branch(pb, conv_w, conv_b, dt_bias, a_log, d_skip, norm_w):
    bsz, seq, cols = pb.shape
    pad16 = lambda t: jnp.pad(t, (0, LANES - SSD_HEADS)).reshape(1, LANES)
    head_of_lane64 = jnp.arange(BRANCH_W) // SSD_HEAD
    head_of_lane128 = jnp.arange(SSD_HEADS * LANES) // LANES
    e64 = (jnp.arange(LANES)[:, None] == head_of_lane64[None, :]).astype(BF16)
    e128 = (jnp.arange(LANES)[:, None] == head_of_lane128[None, :]).astype(BF16)
    d_full = jnp.repeat(d_skip, SSD_HEAD).reshape(1, BRANCH_W)
    const = lambda shape: pl.BlockSpec(shape, lambda b, c: (0, 0))
    return pl.pallas_call(
        _ssd_kernel,
        grid=(bsz, seq // SSD_CHUNK),
        in_specs=[pl.BlockSpec((1, SSD_CHUNK, cols), lambda b, c: (b, c, 0)),
                  const((SSD_CONV, SSD_XBC)), const((1, SSD_XBC)), const((1, LANES)),
                  const((1, LANES)), const((1, BRANCH_W)), const((1, BRANCH_W)),
                  const((LANES, BRANCH_W)), const((LANES, SSD_HEADS * LANES))],
        out_specs=pl.BlockSpec((1, SSD_CHUNK, BRANCH_W), lambda b, c: (b, c, 0)),
        out_shape=jax.ShapeDtypeStruct((bsz, seq, BRANCH_W), BF16),
        scratch_shapes=[pltpu.VMEM((HALO + SSD_CHUNK, SSD_XBC), F32),
                        pltpu.VMEM((SSD_GROUPS, SSD_STATE, BRANCH_W // SSD_GROUPS), F32)],
        compiler_params=_cparams(("parallel", "arbitrary")),
    )(pb, conv_w, conv_b.reshape(1, -1), pad16(dt_bias), pad16(a_log), d_full,
      norm_w.reshape(1, -1), e64, e128)


def _sconv_kernel(pc_ref, cw_ref, o_ref, ext_ref):
    t = pc_ref.shape[1]

    @pl.when(pl.program_id(1) == 0)
    def _():
        ext_ref[0:HALO, :] = jnp.zeros((HALO, BRANCH_W), F32)

    pc = pc_ref[0]
    bg = pc[:, 0:BRANCH_W]
    u = pc[:, BRANCH_W:2 * BRANCH_W] * pc[:, 2 * BRANCH_W:3 * BRANCH_W]
    gate = _silu(pc[:, 3 * BRANCH_W:4 * BRANCH_W])
    ext_ref[HALO:HALO + t, :] = u
    conv = cw_ref[SC_CONV - 1:SC_CONV, :] * u
    for kk in range(SC_CONV - 1):
        back = SC_CONV - 1 - kk
        conv = conv + cw_ref[kk:kk + 1, :] * ext_ref[pl.ds(HALO - back, t), :]
    ext_ref[0:HALO, :] = u[t - HALO:t, :]
    o_ref[0] = (bg * conv * gate).astype(o_ref.dtype)


def _sconv_branch(pc, conv_w, tm=256):
    bsz, seq, cols = pc.shape
    return pl.pallas_call(
        _sconv_kernel,
        grid=(bsz, seq // tm),
        in_specs=[pl.BlockSpec((1, tm, cols), lambda b, c: (b, c, 0)),
                  pl.BlockSpec((SC_CONV, BRANCH_W), lambda b, c: (0, 0))],
        out_specs=pl.BlockSpec((1, tm, BRANCH_W), lambda b, c: (b, c, 0)),
        out_shape=jax.ShapeDtypeStruct((bsz, seq, BRANCH_W), BF16),
        scratch_shapes=[pltpu.VMEM((HALO + tm, BRANCH_W), F32)],
        compiler_params=_cparams(("parallel", "arbitrary")),
    )(pc, conv_w)


def _merge_kernel(oa_ref, ob_ref, oc_ref, ga_ref, gb_ref, gc_ref, wb_ref, o_ref):
    acc = _sigmoid(ga_ref[...]) * _dot(oa_ref[...].astype(BF16), wb_ref[0])
    acc = acc + _sigmoid(gb_ref[...]) * _dot(ob_ref[...], wb_ref[1])
    acc = acc + _sigmoid(gc_ref[...]) * _dot(oc_ref[...], wb_ref[2])
    o_ref[...] = acc.astype(o_ref.dtype)


def _merge(oa, ob, oc, pg, wb, tm=512, tn=1024):
    m = oa.shape[0]
    nj = D_MODEL // tn
    o_spec = pl.BlockSpec((tm, BRANCH_W), lambda i, j: (i, 0))
    g_spec = lambda nb: pl.BlockSpec((tm, tn), lambda i, j: (i, nb * nj + j))
    return pl.pallas_call(
        _merge_kernel,
        grid=(m // tm, nj),
        in_specs=[o_spec, o_spec, o_spec, g_spec(0), g_spec(1), g_spec(2),
                  pl.BlockSpec((N_BRANCH, BRANCH_W, tn), lambda i, j: (0, 0, j))],
        out_specs=pl.BlockSpec((tm, tn), lambda i, j: (i, j)),
        out_shape=jax.ShapeDtypeStruct((m, D_MODEL), BF16),
        compiler_params=_cparams(("parallel", "parallel")),
    )(oa, ob, oc, pg, pg, pg, wb)


def _layer(x2, bsz, seq, norm_w, w_in, rw_mu, rw_w0, rw_w2, rw_a0, rw_a2, rw_kk, rw_ka, rw_rk,
           rw_ln_w, rw_ln_b, ssd_conv_w, ssd_conv_b, ssd_dt_bias, ssd_a_log, ssd_d,
           ssd_norm_w, sc_conv_w, w_branch, w_out):
    m = bsz * seq
    c0, c1, c2 = RW_COLS, RW_COLS + SSD_COLS, RW_COLS + SSD_COLS + SC_COLS
    w_a = w_in[:, :c0].astype(BF16)
    w_b = jnp.pad(w_in[:, c0:c1], ((0, 0), (0, SSD_COLS_PAD - SSD_COLS))).astype(BF16)
    w_c = w_in[:, c1:c2].astype(BF16)
    w_g = w_in[:, c2:].astype(BF16)
    zeros = jnp.zeros((RW_RANK, BRANCH_W), F32)
    w2a2 = jnp.concatenate([jnp.concatenate([rw_w2, zeros], axis=1),
                            jnp.concatenate([zeros, rw_a2], axis=1)], axis=0).astype(BF16)

    h = _rmsnorm(x2, norm_w, BF16)
    pa = _matmul(h, w_a, F32, tm=1024, tn=1408)
    pb = _matmul(h, w_b, F32, tm=1024, tn=640)
    pc = _matmul(h, w_c, F32, tm=1024, tn=1024)
    pg = _matmul(h, w_g, F32, tm=1024, tn=1024)

    oa = _rwkv_branch(pa.reshape(bsz, seq, RW_COLS), rw_mu, rw_w0, w2a2, rw_a0, rw_kk, rw_ka,
                      rw_rk.reshape(-1), rw_ln_w, rw_ln_b)
    ob = _ssd_branch(pb.reshape(bsz, seq, SSD_COLS_PAD), ssd_conv_w, ssd_conv_b, ssd_dt_bias,
                     ssd_a_log, ssd_d, ssd_norm_w)
    oc = _sconv_branch(pc.reshape(bsz, seq, SC_COLS), sc_conv_w)

    merged = _merge(oa.reshape(m, BRANCH_W), ob.reshape(m, BRANCH_W), oc.reshape(m, BRANCH_W),
                    pg, w_branch.astype(BF16))
    return _matmul_residual(merged, w_out.astype(BF16), x2, tm=512, tn=1024)


def kernel(x, norm_w, w_in, rw_mu, rw_w0, rw_w2, rw_a0, rw_a2, rw_kk, rw_ka, rw_rk, rw_ln_w,
           rw_ln_b, ssd_conv_w, ssd_conv_b, ssd_dt_bias, ssd_a_log, ssd_d, ssd_norm_w,
           sc_conv_w, w_branch, w_out, final_norm_w):
    bsz, seq, d = x.shape
    x2 = x.reshape(bsz * seq, d)
    for l in range(DEPTH):
        x2 = _layer(x2, bsz, seq, norm_w[l], w_in[l], rw_mu[l], rw_w0[l], rw_w2[l], rw_a0[l],
                    rw_a2[l], rw_kk[l], rw_ka[l], rw_rk[l], rw_ln_w[l], rw_ln_b[l],
                    ssd_conv_w[l], ssd_conv_b[l], ssd_dt_bias[l], ssd_a_log[l], ssd_d[l],
                    ssd_norm_w[l], sc_conv_w[l], w_branch[l], w_out[l])
    return _rmsnorm(x2, final_norm_w, F32).reshape(bsz, seq, d)
```

```python
import functools
import math

import jax
import jax.numpy as jnp
from jax import lax
from jax.experimental import pallas as pl
from jax.experimental.pallas import tpu as pltpu

F32 = jnp.float32
BF16 = jnp.bfloat16

D_MODEL = 2048
DEPTH = 2
BRANCH_W = D_MODEL // 2
N_BRANCH = 3
RW_HEAD = 64
RW_HEADS = BRANCH_W // RW_HEAD
RW_RANK = 64
RW_LN_EPS = 64e-5
RW_COLS = 4 * BRANCH_W + 2 * RW_RANK
SSD_HEAD = 64
SSD_HEADS = BRANCH_W // SSD_HEAD
SSD_GROUPS = 4
SSD_STATE = 128
SSD_CONV = 4
SSD_XBC = BRANCH_W + 2 * SSD_GROUPS * SSD_STATE
SSD_COLS = BRANCH_W + SSD_XBC + SSD_HEADS
SSD_NORM_EPS = 1e-5
SC_CONV = 3
SC_COLS = 4 * BRANCH_W
GATE_COLS = N_BRANCH * D_MODEL
NORM_EPS = 1e-6

LANES = 128
SSD_COLS_PAD = 3200
RW_CHUNK = 64
SSD_CHUNK = 128
HALO = 8
VMEM_LIMIT = 48 * 1024 * 1024


def _cparams(sem):
    return pltpu.CompilerParams(dimension_semantics=sem, vmem_limit_bytes=VMEM_LIMIT)


def _sigmoid(x):
    return 1.0 / (1.0 + jnp.exp(-x))


def _silu(x):
    return x * _sigmoid(x)


def _softplus(x):
    return jnp.maximum(x, 0.0) + jnp.log1p(jnp.exp(-jnp.abs(x)))


def _dot(a, b):
    return jnp.dot(a, b, preferred_element_type=F32)


def _dot_nt(a, b):
    return lax.dot_general(a, b, (((1,), (1,)), ((), ())), preferred_element_type=F32)


def _dot_tn(a, b):
    return lax.dot_general(a, b, (((0,), (0,)), ((), ())), preferred_element_type=F32)


def _split3(x):
    hi = x.astype(BF16)
    r1 = x - hi.astype(F32)
    mid = r1.astype(BF16)
    lo = (r1 - mid.astype(F32)).astype(BF16)
    return hi, mid, lo


def _dot3(x, m):
    hi, mid, lo = _split3(x)
    return _dot(hi, m) + _dot(mid, m) + _dot(lo, m)


def _dot3_left(m, x):
    hi, mid, lo = _split3(x)
    return _dot(m, hi) + _dot(m, mid) + _dot(m, lo)


def _rmsnorm_kernel(x_ref, w_ref, o_ref):
    x = x_ref[...]
    ms = jnp.mean(x * x, axis=-1, keepdims=True)
    o_ref[...] = (x * lax.rsqrt(ms + NORM_EPS) * w_ref[...]).astype(o_ref.dtype)


def _rmsnorm(x, w, out_dtype, tm=512):
    m, d = x.shape
    return pl.pallas_call(
        _rmsnorm_kernel,
        grid=(m // tm,),
        in_specs=[pl.BlockSpec((tm, d), lambda i: (i, 0)),
                  pl.BlockSpec((1, d), lambda i: (0, 0))],
        out_specs=pl.BlockSpec((tm, d), lambda i: (i, 0)),
        out_shape=jax.ShapeDtypeStruct((m, d), out_dtype),
        compiler_params=_cparams(("parallel",)),
    )(x, w.reshape(1, d))


def _matmul_kernel(a_ref, b_ref, o_ref):
    o_ref[...] = _dot(a_ref[...], b_ref[...]).astype(o_ref.dtype)


def _matmul(a, b, out_dtype, tm, tn):
    m, k = a.shape
    _, n = b.shape
    return pl.pallas_call(
        _matmul_kernel,
        grid=(m // tm, n // tn),
        in_specs=[pl.BlockSpec((tm, k), lambda i, j: (i, 0)),
                  pl.BlockSpec((k, tn), lambda i, j: (0, j))],
        out_specs=pl.BlockSpec((tm, tn), lambda i, j: (i, j)),
        out_shape=jax.ShapeDtypeStruct((m, n), out_dtype),
        compiler_params=_cparams(("parallel", "parallel")),
    )(a, b)


def _matmul_residual_kernel(a_ref, b_ref, x_ref, o_ref):
    o_ref[...] = x_ref[...] + _dot(a_ref[...], b_ref[...])


def _matmul_residual(a, b, x, tm, tn):
    m, k = a.shape
    _, n = b.shape
    return pl.pallas_call(
        _matmul_residual_kernel,
        grid=(m // tm, n // tn),
        in_specs=[pl.BlockSpec((tm, k), lambda i, j: (i, 0)),
                  pl.BlockSpec((k, tn), lambda i, j: (0, j)),
                  pl.BlockSpec((tm, tn), lambda i, j: (i, j))],
        out_specs=pl.BlockSpec((tm, tn), lambda i, j: (i, j)),
        out_shape=jax.ShapeDtypeStruct((m, n), F32),
        compiler_params=_cparams(("parallel", "parallel")),
    )(a, b, x)


def _rwkv_kernel(pa_ref, mu_ref, w0_ref, w2a2_ref, a0_ref, kk_ref, ka_ref, rk_ref,
                 lnw_ref, lnb_ref, o_ref, prev_ref, s_ref):
    c = RW_CHUNK
    n = RW_HEAD

    @pl.when(pl.program_id(1) == 0)
    def _():
        prev_ref[...] = jnp.zeros_like(prev_ref)
        s_ref[...] = jnp.zeros_like(s_ref)

    p = pa_ref[0]
    row = lax.broadcasted_iota(jnp.int32, p.shape, 0)
    shifted = jnp.where(row == 0, prev_ref[...], pltpu.roll(p, 1, axis=0))
    prev_ref[...] = p[c - 1:c, :]
    ps = p + mu_ref[...] * (shifted - p)

    r = ps[:, 0:BRANCH_W]
    k = ps[:, BRANCH_W:2 * BRANCH_W]
    v = ps[:, 2 * BRANCH_W:3 * BRANCH_W]
    gate = _silu(ps[:, 3 * BRANCH_W:4 * BRANCH_W])
    z = ps[:, 4 * BRANCH_W:4 * BRANCH_W + 2 * RW_RANK]
    zlane = lax.broadcasted_iota(jnp.int32, z.shape, 1)
    zz = jnp.where(zlane < RW_RANK, jnp.tanh(z), z).astype(BF16)
    proj = _dot(zz, w2a2_ref[...])
    ld = -_sigmoid(w0_ref[...] + proj[:, :BRANCH_W]) * math.exp(-0.5)
    a = _sigmoid(a0_ref[...] + proj[:, BRANCH_W:])
    kkv = k * kk_ref[...]
    k2 = k * (1.0 + (a - 1.0) * ka_ref[...])
    rkr = r * k2 * rk_ref[...]

    ci = lax.broadcasted_iota(jnp.int32, (c, c), 0)
    cj = lax.broadcasted_iota(jnp.int32, (c, c), 1)
    strict = ci > cj
    incl = ci >= cj
    tril = jnp.where(incl, 1.0, 0.0).astype(BF16)
    eye = jnp.where(ci == cj, 1.0, 0.0)
    cum = _dot3_left(tril, ld)
    tot = cum[c - 1:c, :]
    e_cum = jnp.exp(cum)
    e_ncum = jnp.exp(-cum)
    e_cumx = jnp.exp(cum - ld)
    e_tot = jnp.exp(tot - cum)
    g_c = jnp.exp(tot)

    lvl_masks = []
    shift = 0
    while (1 << shift) < c:
        same_big = (ci >> (shift + 1)) == (cj >> (shift + 1))
        diff_small = (ci >> shift) != (cj >> shift)
        lvl_masks.append(jnp.logical_and(same_big, diff_small))
        shift += 1

    heads = range(RW_HEADS)
    sls = [slice(h * n, (h + 1) * n) for h in heads]
    kh, b_h = [], []
    for sl in sls:
        kkv_h = kkv[:, sl]
        ss = jnp.sum(kkv_h * kkv_h, axis=-1, keepdims=True)
        kh.append(kkv_h * (1.0 / jnp.maximum(jnp.sqrt(ss), 1e-12)))
        b_h.append(kh[-1] * a[:, sl])
    kt = [(kh[h] * e_cumx[:, sls[h]]).astype(BF16) for h in heads]
    bt = [(b_h[h] * e_ncum[:, sls[h]]).astype(BF16) for h in heads]
    kkt = [(k2[:, sl] * e_ncum[:, sl]).astype(BF16) for sl in sls]
    rt = [r[:, sl] * e_cum[:, sl] for sl in sls]
    b_e = [(b_h[h] * e_tot[:, sls[h]]).astype(BF16) for h in heads]
    k_e = [(k2[:, sl] * e_tot[:, sl]).astype(BF16) for sl in sls]
    v_b = [v[:, sl].astype(BF16) for sl in sls]

    x = [jnp.concatenate([kt[h], rt[h].astype(BF16)], axis=0) for h in heads]
    gb = [_dot_nt(x[h], bt[h]) for h in heads]
    gk = [_dot_nt(x[h], kkt[h]) for h in heads]
    l_m = [jnp.where(strict, gb[h][:c], 0.0) for h in heads]
    arb = [jnp.where(incl, gb[h][c:], 0.0).astype(BF16) for h in heads]
    akk = [jnp.where(strict, gk[h][:c], 0.0).astype(BF16) for h in heads]
    ark = [jnp.where(incl, gk[h][c:], 0.0).astype(BF16) for h in heads]

    minv = [eye - jnp.where(lvl_masks[0], l_m[h], 0.0) for h in heads]
    for m in lvl_masks[1:]:
        mb = [minv[h].astype(BF16) for h in heads]
        t1 = [_dot(mb[h], jnp.where(m, l_m[h], 0.0).astype(BF16)).astype(BF16) for h in heads]
        minv = [minv[h] - _dot(t1[h], mb[h]) for h in heads]
    mb = [minv[h].astype(BF16) for h in heads]

    av = [_dot(akk[h], v_b[h]).astype(BF16) for h in heads]
    w_m = [_dot(mb[h], kt[h]).astype(BF16) for h in heads]
    u_m = [_dot(mb[h], av[h]).astype(BF16) for h in heads]
    t_m = [(-_dot_tn(w_m[h], b_e[h])).astype(BF16) for h in heads]
    g_m = [_dot_tn(v_b[h], k_e[h]) - _dot_tn(u_m[h], b_e[h]) for h in heads]
    q_m = [(rt[h] - _dot(arb[h], w_m[h])).astype(BF16) for h in heads]
    y_i = [_dot(ark[h], v_b[h]) - _dot(arb[h], u_m[h]) for h in heads]

    s_old = [s_ref[h] for h in heads]
    s_b = [s_old[h].astype(BF16) for h in heads]
    y = [_dot_nt(q_m[h], s_b[h]) + y_i[h] for h in heads]
    s_new = [s_old[h] * g_c[:, sls[h]] + _dot(s_b[h], t_m[h]) + g_m[h] for h in heads]

    outs = []
    for h in heads:
        sl = sls[h]
        mean = jnp.mean(y[h], axis=-1, keepdims=True)
        yc = y[h] - mean
        var = jnp.mean(yc * yc, axis=-1, keepdims=True)
        yn = yc * lax.rsqrt(var + RW_LN_EPS) * lnw_ref[:, sl] + lnb_ref[:, sl]
        bonus = jnp.sum(rkr[:, sl], axis=-1, keepdims=True) * v[:, sl]
        outs.append((yn + bonus) * gate[:, sl])
    for h in heads:
        s_ref[h] = s_new[h]
    for h in heads:
        o_ref[0, :, sls[h]] = outs[h]


def _rwkv_branch(pa, mu, w0, w2a2, a0, kk, ka, rk, lnw, lnb):
    bsz, seq, cols = pa.shape
    row = lambda t: t.reshape(1, -1)
    vec = lambda width: pl.BlockSpec((1, width), lambda b, c: (0, 0))
    return pl.pallas_call(
        _rwkv_kernel,
        grid=(bsz, seq // RW_CHUNK),
        in_specs=[pl.BlockSpec((1, RW_CHUNK, cols), lambda b, c: (b, c, 0)),
                  vec(cols), vec(BRANCH_W),
                  pl.BlockSpec((2 * RW_RANK, 2 * BRANCH_W), lambda b, c: (0, 0)),
                  vec(BRANCH_W), vec(BRANCH_W), vec(BRANCH_W), vec(BRANCH_W),
                  vec(BRANCH_W), vec(BRANCH_W)],
        out_specs=pl.BlockSpec((1, RW_CHUNK, BRANCH_W), lambda b, c: (b, c, 0)),
        out_shape=jax.ShapeDtypeStruct((bsz, seq, BRANCH_W), F32),
        scratch_shapes=[pltpu.VMEM((1, cols), F32),
                        pltpu.VMEM((RW_HEADS, RW_HEAD, RW_HEAD), F32)],
        compiler_params=_cparams(("parallel", "arbitrary")),
    )(pa, row(mu), row(w0), w2a2, row(a0), row(kk), row(ka), row(rk), row(lnw), row(lnb))


def _ssd_kernel(pb_ref, cw_ref, cb_ref, dtb_ref, alog_ref, d_ref, nw_ref, e64_ref, e128_ref,
                o_ref, ext_ref, st_ref):
    l = SSD_CHUNK
    gw = BRANCH_W // SSD_GROUPS
    hpg = SSD_HEADS // SSD_GROUPS

    @pl.when(pl.program_id(1) == 0)
    def _():
        ext_ref[0:HALO, :] = jnp.zeros((HALO, SSD_XBC), F32)
        st_ref[...] = jnp.zeros_like(st_ref)

    pb = pb_ref[0]
    zgate = _silu(pb[:, 0:BRANCH_W])
    xbc = pb[:, BRANCH_W:BRANCH_W + SSD_XBC]
    dts = pb[:, BRANCH_W + SSD_XBC:BRANCH_W + SSD_XBC + LANES]

    ext_ref[HALO:HALO + l, :] = xbc
    conv = cb_ref[...] + cw_ref[SSD_CONV - 1:SSD_CONV, :] * xbc
    for kk in range(SSD_CONV - 1):
        back = SSD_CONV - 1 - kk
        conv = conv + cw_ref[kk:kk + 1, :] * ext_ref[pl.ds(HALO - back, l), :]
    ext_ref[0:HALO, :] = xbc[l - HALO:l, :]
    xa = _silu(conv)
    xs = xa[:, 0:BRANCH_W]
    bm = xa[:, BRANCH_W:BRANCH_W + SSD_GROUPS * SSD_STATE]
    cm = xa[:, BRANCH_W + SSD_GROUPS * SSD_STATE:]

    dt16 = _softplus(dts + dtb_ref[...])
    da16 = dt16 * (-jnp.exp(alog_ref[...]))
    ri = lax.broadcasted_iota(jnp.int32, (l, l), 0)
    rj = lax.broadcasted_iota(jnp.int32, (l, l), 1)
    causal = ri >= rj
    tril = jnp.where(causal, 1.0, 0.0).astype(BF16)
    acs16 = _dot3_left(tril, da16)
    acs_t = acs16.T
    e64 = e64_ref[...]
    dt_full = _dot3(dt16, e64)
    acs_full = _dot3(acs16, e64)
    acs_b = _dot3(acs16, e128_ref[...])
    tot_full = acs_full[l - 1:l, :]
    dte = jnp.exp(tot_full - acs_full)
    eacs = jnp.exp(acs_full)
    dec = jnp.exp(tot_full)
    xdt = xs * dt_full
    xdte = (xdt * dte).astype(BF16)
    glane = lax.broadcasted_iota(jnp.int32, (l, gw), 1) // SSD_HEAD

    for g in range(SSD_GROUPS):
        gs = slice(g * gw, (g + 1) * gw)
        c_g = cm[:, g * SSD_STATE:(g + 1) * SSD_STATE].astype(BF16)
        b_g = bm[:, g * SSD_STATE:(g + 1) * SSD_STATE].astype(BF16)
        scores = _dot_nt(c_g, b_g)
        xdt_g = xdt[:, gs]
        p_parts = []
        x_parts = []
        for jj in range(hpg):
            j = g * hpg + jj
            diff = acs_b[:, j * LANES:(j + 1) * LANES] - acs_t[j:j + 1, :]
            seg = jnp.exp(jnp.where(causal, diff, -jnp.inf))
            p_parts.append((scores * seg).astype(BF16))
            x_parts.append(jnp.where(glane == jj, xdt_g, 0.0).astype(BF16))
        y_diag = _dot(jnp.concatenate(p_parts, axis=1), jnp.concatenate(x_parts, axis=0))
        st = st_ref[g]
        y_off = _dot(c_g, st.astype(BF16)) * eacs[:, gs]
        st_ref[g] = st * dec[:, gs] + _dot_tn(b_g, xdte[:, gs])
        y = (y_diag + y_off + d_ref[:, gs] * xs[:, gs]) * zgate[:, gs]
        ms = jnp.mean(y * y, axis=-1, keepdims=True)
        o_ref[0, :, gs] = (y * lax.rsqrt(ms + SSD_NORM_EPS) * nw_ref[:, gs]).astype(o_ref.dtype)


def _ssd_branch(pb, conv_w, conv_b, dt_bias, a_log, d_skip, norm_w):
    bsz, seq, cols = pb.shape
    pad16 = lambda t: jnp.pad(t, (0, LANES - SSD_HEADS)).reshape(1, LANES)
    head_of_lane64 = jnp.arange(BRANCH_W) // SSD_HEAD
    head_of_lane128 = jnp.arange(SSD_HEADS * LANES) // LANES
    e64 = (jnp.arange(LANES)[:, None] == head_of_lane64[None, :]).astype(BF16)
    e128 = (jnp.arange(LANES)[:, None] == head_of_lane128[None, :]).astype(BF16)
    d_full = jnp.repeat(d_skip, SSD_HEAD).reshape(1, BRANCH_W)
    const = lambda shape: pl.BlockSpec(shape, lambda b, c: (0, 0))
    return pl.pallas_call(
        _ssd_kernel,
        grid=(bsz, seq // SSD_CHUNK),
        in_specs=[pl.BlockSpec((1, SSD_CHUNK, cols), lambda b, c: (b, c, 0)),
                  const((SSD_CONV, SSD_XBC)), const((1, SSD_XBC)), const((1, LANES)),
                  const((1, LANES)), const((1, BRANCH_W)), const((1, BRANCH_W)),
                  const((LANES, BRANCH_W)), const((LANES, SSD_HEADS * LANES))],
        out_specs=pl.BlockSpec((1, SSD_CHUNK, BRANCH_W), lambda b, c: (b, c, 0)),
        out_shape=jax.ShapeDtypeStruct((bsz, seq, BRANCH_W), BF16),
        scratch_shapes=[pltpu.VMEM((HALO + SSD_CHUNK, SSD_XBC), F32),
                        pltpu.VMEM((SSD_GROUPS, SSD_STATE, BRANCH_W // SSD_GROUPS), F32)],
        compiler_params=_cparams(("parallel", "arbitrary")),
    )(pb, conv_w, conv_b.reshape(1, -1), pad16(dt_bias), pad16(a_log), d_full,
      norm_w.reshape(1, -1), e64, e128)


def _sconv_kernel(pc_ref, cw_ref, o_ref, ext_ref):
    t = pc_ref.shape[1]

    @pl.when(pl.program_id(1) == 0)
    def _():
        ext_ref[0:HALO, :] = jnp.zeros((HALO, BRANCH_W), F32)

    pc = pc_ref[0]
    bg = pc[:, 0:BRANCH_W]
    u = pc[:, BRANCH_W:2 * BRANCH_W] * pc[:, 2 * BRANCH_W:3 * BRANCH_W]
    gate = _silu(pc[:, 3 * BRANCH_W:4 * BRANCH_W])
    ext_ref[HALO:HALO + t, :] = u
    conv = cw_ref[SC_CONV - 1:SC_CONV, :] * u
    for kk in range(SC_CONV - 1):
        back = SC_CONV - 1 - kk
        conv = conv + cw_ref[kk:kk + 1, :] * ext_ref[pl.ds(HALO - back, t), :]
    ext_ref[0:HALO, :] = u[t - HALO:t, :]
    o_ref[0] = (bg * conv * gate).astype(o_ref.dtype)


def _sconv_branch(pc, conv_w, tm=256):
    bsz, seq, cols = pc.shape
    return pl.pallas_call(
        _sconv_kernel,
        grid=(bsz, seq // tm),
        in_specs=[pl.BlockSpec((1, tm, cols), lambda b, c: (b, c, 0)),
                  pl.BlockSpec((SC_CONV, BRANCH_W), lambda b, c: (0, 0))],
        out_specs=pl.BlockSpec((1, tm, BRANCH_W), lambda b, c: (b, c, 0)),
        out_shape=jax.ShapeDtypeStruct((bsz, seq, BRANCH_W), BF16),
        scratch_shapes=[pltpu.VMEM((HALO + tm, BRANCH_W), F32)],
        compiler_params=_cparams(("parallel", "arbitrary")),
    )(pc, conv_w)


def _merge_kernel(oa_ref, ob_ref, oc_ref, ga_ref, gb_ref, gc_ref, wb_ref, o_ref):
    acc = _sigmoid(ga_ref[...]) * _dot(oa_ref[...].astype(BF16), wb_ref[0])
    acc = acc + _sigmoid(gb_ref[...]) * _dot(ob_ref[...], wb_ref[1])
    acc = acc + _sigmoid(gc_ref[...]) * _dot(oc_ref[...], wb_ref[2])
    o_ref[...] = acc.astype(o_ref.dtype)


def _merge(oa, ob, oc, pg, wb, tm=512, tn=1024):
    m = oa.shape[0]
    nj = D_MODEL // tn
    o_spec = pl.BlockSpec((tm, BRANCH_W), lambda i, j: (i, 0))
    g_spec = lambda nb: pl.BlockSpec((tm, tn), lambda i, j: (i, nb * nj + j))
    return pl.pallas_call(
        _merge_kernel,
        grid=(m // tm, nj),
        in_specs=[o_spec, o_spec, o_spec, g_spec(0), g_spec(1), g_spec(2),
                  pl.BlockSpec((N_BRANCH, BRANCH_W, tn), lambda i, j: (0, 0, j))],
        out_specs=pl.BlockSpec((tm, tn), lambda i, j: (i, j)),
        out_shape=jax.ShapeDtypeStruct((m, D_MODEL), BF16),
        compiler_params=_cparams(("parallel", "parallel")),
    )(oa, ob, oc, pg, pg, pg, wb)


def _layer(x2, bsz, seq, norm_w, w_in, rw_mu, rw_w0, rw_w2, rw_a0, rw_a2, rw_kk, rw_ka, rw_rk,
           rw_ln_w, rw_ln_b, ssd_conv_w, ssd_conv_b, ssd_dt_bias, ssd_a_log, ssd_d,
           ssd_norm_w, sc_conv_w, w_branch, w_out):
    m = bsz * seq
    c0, c1, c2 = RW_COLS, RW_COLS + SSD_COLS, RW_COLS + SSD_COLS + SC_COLS
    w_a = w_in[:, :c0].astype(BF16)
    w_b = jnp.pad(w_in[:, c0:c1], ((0, 0), (0, SSD_COLS_PAD - SSD_COLS))).astype(BF16)
    w_c = w_in[:, c1:c2].astype(BF16)
    w_g = w_in[:, c2:].astype(BF16)
    zeros = jnp.zeros((RW_RANK, BRANCH_W), F32)
    w2a2 = jnp.concatenate([jnp.concatenate([rw_w2, zeros], axis=1),
                            jnp.concatenate([zeros, rw_a2], axis=1)], axis=0).astype(BF16)

    h = _rmsnorm(x2, norm_w, BF16)
    pa = _matmul(h, w_a, F32, tm=1024, tn=1408)
    pb = _matmul(h, w_b, F32, tm=1024, tn=640)
    pc = _matmul(h, w_c, F32, tm=1024, tn=1024)
    pg = _matmul(h, w_g, F32, tm=1024, tn=1024)

    oa = _rwkv_branch(pa.reshape(bsz, seq, RW_COLS), rw_mu, rw_w0, w2a2, rw_a0, rw_kk, rw_ka,
                      rw_rk.reshape(-1), rw_ln_w, rw_ln_b)
    ob = _ssd_branch(pb.reshape(bsz, seq, SSD_COLS_PAD), ssd_conv_w, ssd_conv_b, ssd_dt_bias,
                     ssd_a_log, ssd_d, ssd_norm_w)
    oc = _sconv_branch(pc.reshape(bsz, seq, SC_COLS), sc_conv_w)

    merged = _merge(oa.reshape(m, BRANCH_W), ob.reshape(m, BRANCH_W), oc.reshape(m, BRANCH_W),
                    pg, w_branch.astype(BF16))
    return _matmul_residual(merged, w_out.astype(BF16), x2, tm=512, tn=1024)


def kernel(x, norm_w, w_in, rw_mu, rw_w0, rw_w2, rw_a0, rw_a2, rw_kk, rw_ka, rw_rk, rw_ln_w,
           rw_ln_b, ssd_conv_w, ssd_conv_b, ssd_dt_bias, ssd_a_log, ssd_d, ssd_norm_w,
           sc_conv_w, w_branch, w_out, final_norm_w):
    bsz, seq, d = x.shape
    x2 = x.reshape(bsz * seq, d)
    for l in range(DEPTH):
        x2 = _layer(x2, bsz, seq, norm_w[l], w_in[l], rw_mu[l], rw_w0[l], rw_w2[l], rw_a0[l],
                    rw_a2[l], rw_kk[l], rw_ka[l], rw_rk[l], rw_ln_w[l], rw_ln_b[l],
                    ssd_conv_w[l], ssd_conv_b[l], ssd_dt_bias[l], ssd_a_log[l], ssd_d[l],
                    ssd_norm_w[l], sc_conv_w[l], w_branch[l], w_out[l])
    return _rmsnorm(x2, final_norm_w, F32).reshape(bsz, seq, d)
```

```python
import math

import jax
import jax.numpy as jnp
from jax import lax
from jax.experimental import pallas as pl
from jax.experimental.pallas import tpu as pltpu

F32 = jnp.float32
BF16 = jnp.bfloat16

D_MODEL = 2048
DEPTH = 2
BRANCH_W = D_MODEL // 2
N_BRANCH = 3
RW_HEAD = 64
RW_HEADS = BRANCH_W // RW_HEAD
RW_RANK = 64
RW_LN_EPS = 64e-5
RW_COLS = 4 * BRANCH_W + 2 * RW_RANK
SSD_HEAD = 64
SSD_HEADS = BRANCH_W // SSD_HEAD
SSD_GROUPS = 4
SSD_STATE = 128
SSD_CONV = 4
SSD_XBC = BRANCH_W + 2 * SSD_GROUPS * SSD_STATE
SSD_COLS = BRANCH_W + SSD_XBC + SSD_HEADS
SSD_NORM_EPS = 1e-5
SC_CONV = 3
SC_COLS = 4 * BRANCH_W
GATE_COLS = N_BRANCH * D_MODEL
NORM_EPS = 1e-6

LANES = 128
SSD_COLS_PAD = 3200
RW_CHUNK = 64
RW_STEP = 128
SSD_CHUNK = 128
HALO = 8
VMEM_PHYSICAL = 64 * 1024 * 1024
VMEM_INTERNAL = 8 * 1024 * 1024

assert RW_CHUNK == RW_HEAD and 2 * RW_HEAD == LANES


def _cparams(sem, block_bytes, scratch_bytes=0):
    limit = min(2 * block_bytes + scratch_bytes + VMEM_INTERNAL, VMEM_PHYSICAL - VMEM_INTERNAL)
    return pltpu.CompilerParams(dimension_semantics=sem, vmem_limit_bytes=limit)


def _sigmoid(x):
    return 1.0 / (1.0 + jnp.exp(-x))


def _silu(x):
    return x * _sigmoid(x)


def _softplus(x):
    return jnp.maximum(x, 0.0) + jnp.log1p(jnp.exp(-jnp.abs(x)))


def _dot(a, b):
    return jnp.dot(a, b, preferred_element_type=F32)


def _dot_nt(a, b):
    return lax.dot_general(a, b, (((1,), (1,)), ((), ())), preferred_element_type=F32)


def _dot_tn(a, b):
    return lax.dot_general(a, b, (((0,), (0,)), ((), ())), preferred_element_type=F32)


def _split3(x):
    hi = x.astype(BF16)
    r1 = x - hi.astype(F32)
    mid = r1.astype(BF16)
    lo = (r1 - mid.astype(F32)).astype(BF16)
    return hi, mid, lo


def _dot3(x, m):
    hi, mid, lo = _split3(x)
    return _dot(hi, m) + _dot(mid, m) + _dot(lo, m)


def _dot3_left(m, x):
    hi, mid, lo = _split3(x)
    return _dot(m, hi) + _dot(m, mid) + _dot(m, lo)


def _rmsnorm_kernel(x_ref, w_ref, o_ref):
    x = x_ref[...]
    ms = jnp.mean(x * x, axis=-1, keepdims=True)
    o_ref[...] = (x * lax.rsqrt(ms + NORM_EPS) * w_ref[...]).astype(o_ref.dtype)


def _rmsnorm(x, w, out_dtype, tm=512):
    m, d = x.shape
    blocks = tm * d * (4 + jnp.dtype(out_dtype).itemsize) + d * 4
    return pl.pallas_call(
        _rmsnorm_kernel,
        grid=(m // tm,),
        in_specs=[pl.BlockSpec((tm, d), lambda i: (i, 0)),
                  pl.BlockSpec((1, d), lambda i: (0, 0))],
        out_specs=pl.BlockSpec((tm, d), lambda i: (i, 0)),
        out_shape=jax.ShapeDtypeStruct((m, d), out_dtype),
        compiler_params=_cparams(("parallel",), blocks),
    )(x, w.reshape(1, d))


def _matmul_kernel(a_ref, b_ref, o_ref):
    o_ref[...] = _dot(a_ref[...], b_ref[...]).astype(o_ref.dtype)


def _matmul(a, b, out_dtype, tm, tn):
    m, k = a.shape
    _, n = b.shape
    blocks = tm * k * 2 + k * tn * 2 + tm * tn * jnp.dtype(out_dtype).itemsize
    return pl.pallas_call(
        _matmul_kernel,
        grid=(m // tm, n // tn),
        in_specs=[pl.BlockSpec((tm, k), lambda i, j: (i, 0)),
                  pl.BlockSpec((k, tn), lambda i, j: (0, j))],
        out_specs=pl.BlockSpec((tm, tn), lambda i, j: (i, j)),
        out_shape=jax.ShapeDtypeStruct((m, n), out_dtype),
        compiler_params=_cparams(("parallel", "parallel"), blocks),
    )(a, b)


def _out_proj_kernel(a_ref, w_ref, x_ref, nw_ref, *o_refs):
    xn = x_ref[...] + _dot(a_ref[...], w_ref[...])
    ms = jnp.mean(xn * xn, axis=-1, keepdims=True)
    h_ref = o_refs[-1]
    h_ref[...] = (xn * lax.rsqrt(ms + NORM_EPS) * nw_ref[...]).astype(h_ref.dtype)
    if len(o_refs) == 2:
        o_refs[0][...] = xn


def _out_proj(a, w, x, nw, norm_dtype, keep_residual, tm=256):
    m, k = a.shape
    d = w.shape[1]
    row = lambda i: (i, 0)
    norm_bytes = jnp.dtype(norm_dtype).itemsize
    blocks = tm * k * 2 + k * d * 2 + tm * d * (4 + norm_bytes) + d * 4
    out_specs = [pl.BlockSpec((tm, d), row)]
    out_shape = [jax.ShapeDtypeStruct((m, d), norm_dtype)]
    if keep_residual:
        blocks += tm * d * 4
        out_specs.insert(0, pl.BlockSpec((tm, d), row))
        out_shape.insert(0, jax.ShapeDtypeStruct((m, d), F32))
    return pl.pallas_call(
        _out_proj_kernel,
        grid=(m // tm,),
        in_specs=[pl.BlockSpec((tm, k), row),
                  pl.BlockSpec((k, d), lambda i: (0, 0)),
                  pl.BlockSpec((tm, d), row),
                  pl.BlockSpec((1, d), lambda i: (0, 0))],
        out_specs=out_specs,
        out_shape=out_shape,
        compiler_params=_cparams(("parallel",), blocks),
    )(a, w, x, nw.reshape(1, d))


def _rwkv_kernel(pa_ref, mu_ref, w0_ref, w2a2_ref, a0_ref, kk_ref, ka_ref, rk_ref,
                 lnw_ref, lnb_ref, o_ref, prev_ref, s_ref):
    c = RW_CHUNK
    n = RW_HEAD
    pw = 2 * n
    t = RW_STEP
    nchunks = t // c

    @pl.when(pl.program_id(1) == 0)
    def _():
        prev_ref[...] = jnp.zeros_like(prev_ref)
        s_ref[...] = jnp.zeros_like(s_ref)

    p = pa_ref[0]
    prow = lax.broadcasted_iota(jnp.int32, p.shape, 0)
    shifted = jnp.where(prow == 0, prev_ref[...], pltpu.roll(p, 1, axis=0))
    prev_ref[...] = p[t - 1:t, :]
    ps = p + mu_ref[...] * (shifted - p)

    r = ps[:, 0:BRANCH_W]
    k = ps[:, BRANCH_W:2 * BRANCH_W]
    v = ps[:, 2 * BRANCH_W:3 * BRANCH_W]
    gate = _silu(ps[:, 3 * BRANCH_W:4 * BRANCH_W])
    z = ps[:, 4 * BRANCH_W:4 * BRANCH_W + 2 * RW_RANK]
    zlane = lax.broadcasted_iota(jnp.int32, z.shape, 1)
    zz = jnp.where(zlane < RW_RANK, jnp.tanh(z), z).astype(BF16)
    proj = _dot(zz, w2a2_ref[...])
    ld = -_sigmoid(w0_ref[...] + proj[:, :BRANCH_W]) * math.exp(-0.5)
    a = _sigmoid(a0_ref[...] + proj[:, BRANCH_W:])
    kkv = k * kk_ref[...]
    k2 = k * (1.0 + (a - 1.0) * ka_ref[...])
    rkr = r * k2 * rk_ref[...]

    ti = lax.broadcasted_iota(jnp.int32, (t, t), 0)
    tj = lax.broadcasted_iota(jnp.int32, (t, t), 1)
    same_chunk = (ti // c) == (tj // c)
    tril = jnp.where(jnp.logical_and(same_chunk, ti >= tj), 1.0, 0.0).astype(BF16)
    ld_hi = ld.astype(BF16)
    ld_lo = (ld - ld_hi.astype(F32)).astype(BF16)
    cum = _dot(tril, ld_hi) + _dot(tril, ld_lo)
    e_cum = jnp.exp(cum)
    e_ncum = 1.0 / e_cum
    e_cumx = jnp.exp(cum - ld)

    ci = lax.broadcasted_iota(jnp.int32, (c, pw), 0)
    lane = lax.broadcasted_iota(jnp.int32, (c, pw), 1)
    cj = lane & (c - 1)
    lo = lane < n
    strict = ci > cj
    incl = ci >= cj
    eye = jnp.where(ci == cj, 1.0, 0.0)
    lo_b = jnp.where(lo, 1.0, 0.0).astype(BF16)
    hi_b = jnp.where(lo, 0.0, 1.0).astype(BF16)
    lvl_masks = []
    shift = 0
    while (1 << shift) < c:
        same_big = (ci >> (shift + 1)) == (cj >> (shift + 1))
        diff_small = (ci >> shift) != (cj >> shift)
        lvl_masks.append(jnp.logical_and(same_big, diff_small))
        shift += 1
    bi = lax.broadcasted_iota(jnp.int32, (pw, pw), 0)
    bj = lax.broadcasted_iota(jnp.int32, (pw, pw), 1)
    blockdiag = (bi >= n) == (bj >= n)
    br, bc = bi & (c - 1), bj & (c - 1)
    lvl_bd = []
    for shift in range(1, len(lvl_masks)):
        same_big = (br >> (shift + 1)) == (bc >> (shift + 1))
        diff_small = (br >> shift) != (bc >> shift)
        keep = jnp.logical_and(blockdiag, jnp.logical_and(same_big, diff_small))
        lvl_bd.append(jnp.where(keep, 1.0, 0.0).astype(BF16))

    def bd(y):
        return jnp.concatenate([y * lo_b, y * hi_b], axis=0)

    def half_sums(x):
        s_lo = jnp.sum(jnp.where(lo, x, 0.0), axis=-1, keepdims=True)
        s_hi = jnp.sum(jnp.where(lo, 0.0, x), axis=-1, keepdims=True)
        return jnp.where(lo, s_lo, s_hi)

    pairs = range(RW_HEADS // 2)
    sls = [slice(j * pw, (j + 1) * pw) for j in pairs]
    state = [s_ref[j] for j in pairs]
    outs = []
    for q in range(nchunks):
        rows = slice(q * c, (q + 1) * c)
        blk = lambda arr, sl: arr[rows, sl]
        g_c = jnp.exp(cum[(q + 1) * c - 1:(q + 1) * c, :])
        kh = [blk(kkv, sl) * jnp.minimum(lax.rsqrt(half_sums(blk(kkv, sl) * blk(kkv, sl))), 1e12)
              for sl in sls]
        b2 = [kh[j] * blk(a, sls[j]) for j in pairs]
        e_tot = [blk(e_ncum, sl) * g_c[:, sl] for sl in sls]
        kt = [(kh[j] * blk(e_cumx, sls[j])).astype(BF16) for j in pairs]
        bt = [(b2[j] * blk(e_ncum, sls[j])).astype(BF16) for j in pairs]
        kkt = [(blk(k2, sl) * blk(e_ncum, sl)).astype(BF16) for sl in sls]
        rt = [(blk(r, sl) * blk(e_cum, sl)).astype(BF16) for sl in sls]
        be_n = [(-(b2[j] * e_tot[j])).astype(BF16) for j in pairs]
        ke = [(blk(k2, sls[j]) * e_tot[j]).astype(BF16) for j in pairs]
        vb = [blk(v, sl).astype(BF16) for sl in sls]

        x2 = [jnp.concatenate([kt[j], rt[j]], axis=0) for j in pairs]
        rr = [jnp.concatenate([bt[j] * lo_b, bt[j] * hi_b, kkt[j] * lo_b, kkt[j] * hi_b], axis=0)
              for j in pairs]
        gall = [_dot_nt(x2[j], rr[j]) for j in pairs]
        l_m = [jnp.where(strict, gall[j][:c, :pw], 0.0) for j in pairs]
        akk = [jnp.where(strict, gall[j][:c, pw:], 0.0).astype(BF16) for j in pairs]
        arb_n = [jnp.where(incl, -gall[j][c:, :pw], 0.0).astype(BF16) for j in pairs]
        ark = [jnp.where(incl, gall[j][c:, pw:], 0.0).astype(BF16) for j in pairs]

        minv = [eye - jnp.where(lvl_masks[0], l_m[j], 0.0) for j in pairs]
        l_b = [l_m[j].astype(BF16) for j in pairs]
        l_t = [jnp.concatenate([l_b[j], l_b[j]], axis=0) for j in pairs]
        for m_bd in lvl_bd:
            mb = [minv[j].astype(BF16) for j in pairs]
            t1 = [_dot(mb[j], l_t[j] * m_bd).astype(BF16) for j in pairs]
            minv = [minv[j] - _dot(t1[j], bd(mb[j])) for j in pairs]
        mb = [minv[j].astype(BF16) for j in pairs]

        bdv = [bd(vb[j]) for j in pairs]
        av = [_dot(akk[j], bdv[j]) for j in pairs]
        s_b = [state[j].astype(BF16) for j in pairs]
        ks = [_dot_nt(x2[j], s_b[j]) for j in pairs]
        u_b = [_dot(mb[j], bd((ks[j][:c] + av[j]).astype(BF16))).astype(BF16) for j in pairs]
        y = [ks[j][c:] + _dot(jnp.concatenate([ark[j], arb_n[j]], axis=1),
                              jnp.concatenate([bdv[j], bd(u_b[j])], axis=0)) for j in pairs]
        state = [state[j] * g_c[:, sls[j]]
                 + jnp.where(blockdiag,
                             _dot_tn(jnp.concatenate([vb[j], u_b[j]], axis=0),
                                     jnp.concatenate([ke[j], be_n[j]], axis=0)), 0.0)
                 for j in pairs]

        for j in pairs:
            sl = sls[j]
            yc = y[j] - half_sums(y[j]) * (1.0 / n)
            var = half_sums(yc * yc) * (1.0 / n)
            yn = yc * lax.rsqrt(var + RW_LN_EPS) * lnw_ref[:, sl] + lnb_ref[:, sl]
            bonus = half_sums(blk(rkr, sl)) * blk(v, sl)
            outs.append(((yn + bonus) * blk(gate, sl)).astype(o_ref.dtype))
    for j in pairs:
        s_ref[j] = state[j]
    for q in range(nchunks):
        for j in pairs:
            o_ref[0, q * c:(q + 1) * c, sls[j]] = outs[q * len(sls) + j]


def _rwkv_branch(pa, mu, w0, w2a2, a0, kk, ka, rk, lnw, lnb):
    bsz, seq, cols = pa.shape
    row = lambda t: t.reshape(1, -1)
    vec = lambda width: pl.BlockSpec((1, width), lambda b, c: (0, 0))
    blocks = RW_STEP * cols * 4 + cols * 4 + 8 * BRANCH_W * 4 + w2a2.size * 2 + RW_STEP * BRANCH_W * 2
    scratch = cols * 4 * HALO + (RW_HEADS // 2) * LANES * LANES * 4
    return pl.pallas_call(
        _rwkv_kernel,
        grid=(bsz, seq // RW_STEP),
        in_specs=[pl.BlockSpec((1, RW_STEP, cols), lambda b, c: (b, c, 0)),
                  vec(cols), vec(BRANCH_W),
                  pl.BlockSpec((2 * RW_RANK, 2 * BRANCH_W), lambda b, c: (0, 0)),
                  vec(BRANCH_W), vec(BRANCH_W), vec(BRANCH_W), vec(BRANCH_W),
                  vec(BRANCH_W), vec(BRANCH_W)],
        out_specs=pl.BlockSpec((1, RW_STEP, BRANCH_W), lambda b, c: (b, c, 0)),
        out_shape=jax.ShapeDtypeStruct((bsz, seq, BRANCH_W), BF16),
        scratch_shapes=[pltpu.VMEM((1, cols), F32),
                        pltpu.VMEM((RW_HEADS // 2, LANES, LANES), F32)],
        compiler_params=_cparams(("parallel", "arbitrary"), blocks, scratch),
    )(pa, row(mu), row(w0), w2a2, row(a0), row(kk), row(ka), row(rk), row(lnw), row(lnb))


def _ssd_kernel(pb_ref, cw_ref, cb_ref, dtb_ref, alog_ref, d_ref, nw_ref, e64_ref, e128_ref,
                o_ref, ext_ref, st_ref):
    l = SSD_CHUNK
    gw = BRANCH_W // SSD_GROUPS
    hpg = SSD_HEADS // SSD_GROUPS

    @pl.when(pl.program_id(1) == 0)
    def _():
        ext_ref[0:HALO, :] = jnp.zeros((HALO, SSD_XBC), F32)
        st_ref[...] = jnp.zeros_like(st_ref)

    pb = pb_ref[0]
    zgate = _silu(pb[:, 0:BRANCH_W])
    xbc = pb[:, BRANCH_W:BRANCH_W + SSD_XBC]
    dts = pb[:, BRANCH_W + SSD_XBC:BRANCH_W + SSD_XBC + LANES]

    ext_ref[HALO:HALO + l, :] = xbc
    conv = cb_ref[...] + cw_ref[SSD_CONV - 1:SSD_CONV, :] * xbc
    for kk in range(SSD_CONV - 1):
        back = SSD_CONV - 1 - kk
        conv = conv + cw_ref[kk:kk + 1, :] * ext_ref[pl.ds(HALO - back, l), :]
    ext_ref[0:HALO, :] = xbc[l - HALO:l, :]
    xa = _silu(conv)
    xs = xa[:, 0:BRANCH_W]
    bm = xa[:, BRANCH_W:BRANCH_W + SSD_GROUPS * SSD_STATE]
    cm = xa[:, BRANCH_W + SSD_GROUPS * SSD_STATE:]

    dt16 = _softplus(dts + dtb_ref[...])
    da16 = dt16 * (-jnp.exp(alog_ref[...]))
    ri = lax.broadcasted_iota(jnp.int32, (l, l), 0)
    rj = lax.broadcasted_iota(jnp.int32, (l, l), 1)
    causal = ri >= rj
    tril = jnp.where(causal, 1.0, 0.0).astype(BF16)
    acs16 = _dot3_left(tril, da16)
    acs_t = acs16.T
    e64 = e64_ref[...]
    dt_full = _dot3(dt16, e64)
    acs_full = _dot3(acs16, e64)
    acs_b = _dot3(acs16, e128_ref[...])
    tot_full = acs_full[l - 1:l, :]
    dte = jnp.exp(tot_full - acs_full)
    eacs = jnp.exp(acs_full)
    dec = jnp.exp(tot_full)
    xdt = xs * dt_full
    xdte = (xdt * dte).astype(BF16)
    glane = lax.broadcasted_iota(jnp.int32, (l, gw), 1) // SSD_HEAD

    outs = []
    st_new = []
    for g in range(SSD_GROUPS):
        gs = slice(g * gw, (g + 1) * gw)
        c_g = cm[:, g * SSD_STATE:(g + 1) * SSD_STATE].astype(BF16)
        b_g = bm[:, g * SSD_STATE:(g + 1) * SSD_STATE].astype(BF16)
        scores = _dot_nt(c_g, b_g)
        xdt_g = xdt[:, gs]
        p_parts = []
        x_parts = []
        for jj in range(hpg):
            j = g * hpg + jj
            diff = acs_b[:, j * LANES:(j + 1) * LANES] - acs_t[j:j + 1, :]
            seg = jnp.exp(jnp.where(causal, diff, -jnp.inf))
            p_parts.append((scores * seg).astype(BF16))
            x_parts.append(jnp.where(glane == jj, xdt_g, 0.0).astype(BF16))
        y_diag = _dot(jnp.concatenate(p_parts, axis=1), jnp.concatenate(x_parts, axis=0))
        st = st_ref[g]
        y_off = _dot(c_g, st.astype(BF16)) * eacs[:, gs]
        st_new.append(st * dec[:, gs] + _dot_tn(b_g, xdte[:, gs]))
        y = (y_diag + y_off + d_ref[:, gs] * xs[:, gs]) * zgate[:, gs]
        ms = jnp.mean(y * y, axis=-1, keepdims=True)
        outs.append((y * lax.rsqrt(ms + SSD_NORM_EPS) * nw_ref[:, gs]).astype(o_ref.dtype))
    for g in range(SSD_GROUPS):
        st_ref[g] = st_new[g]
    for g in range(SSD_GROUPS):
        o_ref[0, :, g * gw:(g + 1) * gw] = outs[g]


def _ssd_branch(pb, conv_w, conv_b, dt_bias, a_log, d_skip, norm_w):
    bsz, seq, cols = pb.shape
    pad16 = lambda t: jnp.pad(t, (0, LANES - SSD_HEADS)).reshape(1, LANES)
    head_of_lane64 = jnp.arange(BRANCH_W) // SSD_HEAD
    head_of_lane128 = jnp.arange(SSD_HEADS * LANES) // LANES
    e64 = (jnp.arange(LANES)[:, None] == head_of_lane64[None, :]).astype(BF16)
    e128 = (jnp.arange(LANES)[:, None] == head_of_lane128[None, :]).astype(BF16)
    d_full = jnp.repeat(d_skip, SSD_HEAD).reshape(1, BRANCH_W)
    const = lambda shape: pl.BlockSpec(shape, lambda b, c: (0, 0))
    blocks = (SSD_CHUNK * cols * 4 + (SSD_CONV + 1) * SSD_XBC * 4 + 4 * BRANCH_W * 4
              + e64.size * 2 + e128.size * 2 + SSD_CHUNK * BRANCH_W * 2)
    scratch = (HALO + SSD_CHUNK) * SSD_XBC * 4 + SSD_STATE * BRANCH_W * 4
    return pl.pallas_call(
        _ssd_kernel,
        grid=(bsz, seq // SSD_CHUNK),
        in_specs=[pl.BlockSpec((1, SSD_CHUNK, cols), lambda b, c: (b, c, 0)),
                  const((SSD_CONV, SSD_XBC)), const((1, SSD_XBC)), const((1, LANES)),
                  const((1, LANES)), const((1, BRANCH_W)), const((1, BRANCH_W)),
                  const((LANES, BRANCH_W)), const((LANES, SSD_HEADS * LANES))],
        out_specs=pl.BlockSpec((1, SSD_CHUNK, BRANCH_W), lambda b, c: (b, c, 0)),
        out_shape=jax.ShapeDtypeStruct((bsz, seq, BRANCH_W), BF16),
        scratch_shapes=[pltpu.VMEM((HALO + SSD_CHUNK, SSD_XBC), F32),
                        pltpu.VMEM((SSD_GROUPS, SSD_STATE, BRANCH_W // SSD_GROUPS), F32)],
        compiler_params=_cparams(("parallel", "arbitrary"), blocks, scratch),
    )(pb, conv_w, conv_b.reshape(1, -1), pad16(dt_bias), pad16(a_log), d_full,
      norm_w.reshape(1, -1), e64, e128)


def _sconv_kernel(pc_ref, cw_ref, o_ref, ext_ref):
    t = pc_ref.shape[1]

    @pl.when(pl.program_id(1) == 0)
    def _():
        ext_ref[0:HALO, :] = jnp.zeros((HALO, BRANCH_W), F32)

    pc = pc_ref[0].astype(F32)
    bg = pc[:, 0:BRANCH_W]
    u = pc[:, BRANCH_W:2 * BRANCH_W] * pc[:, 2 * BRANCH_W:3 * BRANCH_W]
    gate = _silu(pc[:, 3 * BRANCH_W:4 * BRANCH_W])
    ext_ref[HALO:HALO + t, :] = u
    conv = cw_ref[SC_CONV - 1:SC_CONV, :] * u
    for kk in range(SC_CONV - 1):
        back = SC_CONV - 1 - kk
        conv = conv + cw_ref[kk:kk + 1, :] * ext_ref[pl.ds(HALO - back, t), :]
    ext_ref[0:HALO, :] = u[t - HALO:t, :]
    o_ref[0] = (bg * conv * gate).astype(o_ref.dtype)


def _sconv_branch(pc, conv_w, tm=512):
    bsz, seq, cols = pc.shape
    blocks = tm * cols * pc.dtype.itemsize + SC_CONV * BRANCH_W * 4 + tm * BRANCH_W * 2
    scratch = (HALO + tm) * BRANCH_W * 4
    return pl.pallas_call(
        _sconv_kernel,
        grid=(bsz, seq // tm),
        in_specs=[pl.BlockSpec((1, tm, cols), lambda b, c: (b, c, 0)),
                  pl.BlockSpec((SC_CONV, BRANCH_W), lambda b, c: (0, 0))],
        out_specs=pl.BlockSpec((1, tm, BRANCH_W), lambda b, c: (b, c, 0)),
        out_shape=jax.ShapeDtypeStruct((bsz, seq, BRANCH_W), BF16),
        scratch_shapes=[pltpu.VMEM((HALO + tm, BRANCH_W), F32)],
        compiler_params=_cparams(("parallel", "arbitrary"), blocks, scratch),
    )(pc, conv_w)


def _merge_kernel(oa_ref, ob_ref, oc_ref, ga_ref, gb_ref, gc_ref, wb_ref, o_ref):
    acc = _sigmoid(ga_ref[...].astype(F32)) * _dot(oa_ref[...], wb_ref[0])
    acc = acc + _sigmoid(gb_ref[...].astype(F32)) * _dot(ob_ref[...], wb_ref[1])
    acc = acc + _sigmoid(gc_ref[...].astype(F32)) * _dot(oc_ref[...], wb_ref[2])
    o_ref[...] = acc.astype(o_ref.dtype)


def _merge(oa, ob, oc, pg, wb, tm=256):
    m = oa.shape[0]
    o_spec = pl.BlockSpec((tm, BRANCH_W), lambda i: (i, 0))
    g_spec = lambda nb: pl.BlockSpec((tm, D_MODEL), lambda i: (i, nb))
    blocks = (N_BRANCH * tm * BRANCH_W * 2 + N_BRANCH * tm * D_MODEL * pg.dtype.itemsize
              + wb.size * 2 + tm * D_MODEL * 2)
    return pl.pallas_call(
        _merge_kernel,
        grid=(m // tm,),
        in_specs=[o_spec, o_spec, o_spec, g_spec(0), g_spec(1), g_spec(2),
                  pl.BlockSpec((N_BRANCH, BRANCH_W, D_MODEL), lambda i: (0, 0, 0))],
        out_specs=pl.BlockSpec((tm, D_MODEL), lambda i: (i, 0)),
        out_shape=jax.ShapeDtypeStruct((m, D_MODEL), BF16),
        compiler_params=_cparams(("parallel",), blocks),
    )(oa, ob, oc, pg, pg, pg, wb)


def _layer(x2, h, bsz, seq, next_norm_w, last, w_in, rw_mu, rw_w0, rw_w2, rw_a0, rw_a2, rw_kk,
           rw_ka, rw_rk, rw_ln_w, rw_ln_b, ssd_conv_w, ssd_conv_b, ssd_dt_bias, ssd_a_log, ssd_d,
           ssd_norm_w, sc_conv_w, w_branch, w_out):
    m = bsz * seq
    c0, c1, c2 = RW_COLS, RW_COLS + SSD_COLS, RW_COLS + SSD_COLS + SC_COLS
    w_a = w_in[:, :c0].astype(BF16)
    w_b = jnp.pad(w_in[:, c0:c1], ((0, 0), (0, SSD_COLS_PAD - SSD_COLS))).astype(BF16)
    w_c = w_in[:, c1:c2].astype(BF16)
    w_g = w_in[:, c2:].astype(BF16)
    zeros = jnp.zeros((RW_RANK, BRANCH_W), F32)
    w2a2 = jnp.concatenate([jnp.concatenate([rw_w2, zeros], axis=1),
                            jnp.concatenate([zeros, rw_a2], axis=1)], axis=0).astype(BF16)

    pa = _matmul(h, w_a, F32, tm=1024, tn=1408)
    pb = _matmul(h, w_b, F32, tm=512, tn=SSD_COLS_PAD)
    pc = _matmul(h, w_c, BF16, tm=1024, tn=2048)
    pg = _matmul(h, w_g, BF16, tm=1024, tn=2048)

    oa = _rwkv_branch(pa.reshape(bsz, seq, RW_COLS), rw_mu, rw_w0, w2a2, rw_a0, rw_kk, rw_ka,
                      rw_rk.reshape(-1), rw_ln_w, rw_ln_b)
    ob = _ssd_branch(pb.reshape(bsz, seq, SSD_COLS_PAD), ssd_conv_w, ssd_conv_b, ssd_dt_bias,
                     ssd_a_log, ssd_d, ssd_norm_w)
    oc = _sconv_branch(pc.reshape(bsz, seq, SC_COLS), sc_conv_w)

    merged = _merge(oa.reshape(m, BRANCH_W), ob.reshape(m, BRANCH_W), oc.reshape(m, BRANCH_W),
                    pg, w_branch.astype(BF16))
    if last:
        (y,) = _out_proj(merged, w_out.astype(BF16), x2, next_norm_w, F32, keep_residual=False)
        return None, y
    return _out_proj(merged, w_out.astype(BF16), x2, next_norm_w, BF16, keep_residual=True)


def kernel(x, norm_w, w_in, rw_mu, rw_w0, rw_w2, rw_a0, rw_a2, rw_kk, rw_ka, rw_rk, rw_ln_w,
           rw_ln_b, ssd_conv_w, ssd_conv_b, ssd_dt_bias, ssd_a_log, ssd_d, ssd_norm_w,
           sc_conv_w, w_branch, w_out, final_norm_w):
    bsz, seq, d = x.shape
    x2 = x.reshape(bsz * seq, d)
    h = _rmsnorm(x2, norm_w[0], BF16)
    for l in range(DEPTH):
        last = l == DEPTH - 1
        next_norm_w = final_norm_w if last else norm_w[l + 1]
        x2, h = _layer(x2, h, bsz, seq, next_norm_w, last, w_in[l], rw_mu[l], rw_w0[l], rw_w2[l],
                       rw_a0[l], rw_a2[l], rw_kk[l], rw_ka[l], rw_rk[l], rw_ln_w[l], rw_ln_b[l],
                       ssd_conv_w[l], ssd_conv_b[l], ssd_dt_bias[l], ssd_a_log[l], ssd_d[l],
                       ssd_norm_w[l], sc_conv_w[l], w_branch[l], w_out[l])
    return h.reshape(bsz, seq, d)
```

```python
import functools
import math

import jax
import jax.numpy as jnp
from jax import lax
from jax.experimental import pallas as pl
from jax.experimental.pallas import tpu as pltpu

F32 = jnp.float32
BF16 = jnp.bfloat16

D_MODEL = 2048
DEPTH = 2
BRANCH_W = D_MODEL // 2
N_BRANCH = 3
RW_HEAD = 64
RW_HEADS = BRANCH_W // RW_HEAD
RW_RANK = 64
RW_LN_EPS = 64e-5
RW_COLS = 4 * BRANCH_W + 2 * RW_RANK
SSD_HEAD = 64
SSD_HEADS = BRANCH_W // SSD_HEAD
SSD_GROUPS = 4
SSD_STATE = 128
SSD_CONV = 4
SSD_XBC = BRANCH_W + 2 * SSD_GROUPS * SSD_STATE
SSD_COLS = BRANCH_W + SSD_XBC + SSD_HEADS
SSD_NORM_EPS = 1e-5
SC_CONV = 3
SC_COLS = 4 * BRANCH_W
GATE_COLS = N_BRANCH * D_MODEL
NORM_EPS = 1e-6

LANES = 128
SSD_COLS_PAD = 3200
RW_CHUNK = 64
RW_STEP = 256
SSD_CHUNK = 128
HALO = 8
VMEM_PHYSICAL = 64 * 1024 * 1024
VMEM_INTERNAL = 8 * 1024 * 1024

assert RW_CHUNK == RW_HEAD and 2 * RW_HEAD == LANES


def _cparams(sem, block_bytes, scratch_bytes=0):
    limit = min(2 * block_bytes + scratch_bytes + VMEM_INTERNAL, VMEM_PHYSICAL - VMEM_INTERNAL)
    return pltpu.CompilerParams(dimension_semantics=sem, vmem_limit_bytes=limit)


def _sigmoid(x):
    return 1.0 / (1.0 + jnp.exp(-x))


def _silu(x):
    return x * _sigmoid(x)


def _softplus(x):
    return jnp.maximum(x, 0.0) + jnp.log1p(jnp.exp(-jnp.abs(x)))


def _dot(a, b):
    return jnp.dot(a, b, preferred_element_type=F32)


def _dot_nt(a, b):
    return lax.dot_general(a, b, (((1,), (1,)), ((), ())), preferred_element_type=F32)


def _dot_tn(a, b):
    return lax.dot_general(a, b, (((0,), (0,)), ((), ())), preferred_element_type=F32)


def _split3(x):
    hi = x.astype(BF16)
    r1 = x - hi.astype(F32)
    mid = r1.astype(BF16)
    lo = (r1 - mid.astype(F32)).astype(BF16)
    return hi, mid, lo


def _dot3(x, m):
    hi, mid, lo = _split3(x)
    return _dot(hi, m) + _dot(mid, m) + _dot(lo, m)


def _dot3_left(m, x):
    hi, mid, lo = _split3(x)
    return _dot(m, hi) + _dot(m, mid) + _dot(m, lo)


def _rmsnorm_kernel(x_ref, w_ref, o_ref):
    x = x_ref[...]
    ms = jnp.mean(x * x, axis=-1, keepdims=True)
    o_ref[...] = (x * lax.rsqrt(ms + NORM_EPS) * w_ref[...]).astype(o_ref.dtype)


def _rmsnorm(x, w, out_dtype, tm=512):
    m, d = x.shape
    blocks = tm * d * (4 + jnp.dtype(out_dtype).itemsize) + d * 4
    return pl.pallas_call(
        _rmsnorm_kernel,
        grid=(m // tm,),
        in_specs=[pl.BlockSpec((tm, d), lambda i: (i, 0)),
                  pl.BlockSpec((1, d), lambda i: (0, 0))],
        out_specs=pl.BlockSpec((tm, d), lambda i: (i, 0)),
        out_shape=jax.ShapeDtypeStruct((m, d), out_dtype),
        compiler_params=_cparams(("parallel",), blocks),
    )(x, w.reshape(1, d))


def _matmul_kernel(a_ref, b_ref, o_ref):
    o_ref[...] = _dot(a_ref[...], b_ref[...]).astype(o_ref.dtype)


def _matmul(a, b, out_dtype, tm, tn):
    m, k = a.shape
    _, n = b.shape
    blocks = tm * k * 2 + k * tn * 2 + tm * tn * jnp.dtype(out_dtype).itemsize
    return pl.pallas_call(
        _matmul_kernel,
        grid=(m // tm, n // tn),
        in_specs=[pl.BlockSpec((tm, k), lambda i, j: (i, 0)),
                  pl.BlockSpec((k, tn), lambda i, j: (0, j))],
        out_specs=pl.BlockSpec((tm, tn), lambda i, j: (i, j)),
        out_shape=jax.ShapeDtypeStruct((m, n), out_dtype),
        compiler_params=_cparams(("parallel", "parallel"), blocks),
    )(a, b)


def _matmul_shift_kernel(a_ref, b_ref, mu_ref, o_ref, carry_ref, *, tiles_per_seq):
    i = pl.program_id(0)
    j = pl.program_id(1)
    p = _dot(a_ref[...], b_ref[...])
    tm = p.shape[0]
    prev_last = jnp.where(lax.rem(i, tiles_per_seq) == 0, 0.0, carry_ref[j, 0:1, :])
    row = lax.broadcasted_iota(jnp.int32, p.shape, 0)
    shifted = jnp.where(row == 0, prev_last, pltpu.roll(p, 1, axis=0))
    carry_ref[j, 0:1, :] = p[tm - 1:tm, :]
    o_ref[...] = p + mu_ref[...] * (shifted - p)


def _matmul_shift(a, b, mu, seq, tm, tn):
    m, k = a.shape
    _, n = b.shape
    assert seq % tm == 0
    blocks = tm * k * 2 + k * tn * 2 + tm * tn * 4 + tn * 4
    scratch = (n // tn) * HALO * tn * 4
    return pl.pallas_call(
        functools.partial(_matmul_shift_kernel, tiles_per_seq=seq // tm),
        grid=(m // tm, n // tn),
        in_specs=[pl.BlockSpec((tm, k), lambda i, j: (i, 0)),
                  pl.BlockSpec((k, tn), lambda i, j: (0, j)),
                  pl.BlockSpec((1, tn), lambda i, j: (0, j))],
        out_specs=pl.BlockSpec((tm, tn), lambda i, j: (i, j)),
        out_shape=jax.ShapeDtypeStruct((m, n), F32),
        scratch_shapes=[pltpu.VMEM((n // tn, HALO, tn), F32)],
        compiler_params=_cparams(("arbitrary", "arbitrary"), blocks, scratch),
    )(a, b, mu.reshape(1, n))


def _split_w_in_kernel(w_ref, a_ref, b_ref, c_ref, g_ref):
    c0, c1, c2 = RW_COLS, RW_COLS + SSD_COLS, RW_COLS + SSD_COLS + SC_COLS
    a_ref[...] = w_ref[:, 0:c0].astype(BF16)
    b_ref[:, SSD_COLS_PAD - LANES:] = jnp.zeros((b_ref.shape[0], LANES), BF16)
    b_ref[:, 0:SSD_COLS] = w_ref[:, c0:c1].astype(BF16)
    c_ref[...] = w_ref[:, c1:c2].astype(BF16)
    g_ref[...] = w_ref[:, c2:].astype(BF16)


def _split_w_in(w_in, tr=128):
    k, n = w_in.shape
    widths = (RW_COLS, SSD_COLS_PAD, SC_COLS, GATE_COLS)
    blocks = tr * n * 4 + tr * sum(widths) * 2
    return pl.pallas_call(
        _split_w_in_kernel,
        grid=(k // tr,),
        in_specs=[pl.BlockSpec((tr, n), lambda i: (i, 0))],
        out_specs=[pl.BlockSpec((tr, w), lambda i: (i, 0)) for w in widths],
        out_shape=[jax.ShapeDtypeStruct((k, w), BF16) for w in widths],
        compiler_params=_cparams(("parallel",), blocks),
    )(w_in)


def _cast_kernel(x_ref, o_ref):
    o_ref[...] = x_ref[...].astype(o_ref.dtype)


def _cast_bf16(x, tr=512):
    m, n = x.shape
    return pl.pallas_call(
        _cast_kernel,
        grid=(m // tr,),
        in_specs=[pl.BlockSpec((tr, n), lambda i: (i, 0))],
        out_specs=pl.BlockSpec((tr, n), lambda i: (i, 0)),
        out_shape=jax.ShapeDtypeStruct((m, n), BF16),
        compiler_params=_cparams(("parallel",), tr * n * 6),
    )(x)


def _rwkv_kernel(pa_ref, w0_ref, w2a2_ref, a0_ref, kk_ref, ka_ref, rk_ref,
                 lnw_ref, lnb_ref, o_ref, s_ref):
    c = RW_CHUNK
    n = RW_HEAD
    pw = 2 * n
    t = RW_STEP
    nchunks = t // c

    @pl.when(pl.program_id(1) == 0)
    def _():
        s_ref[...] = jnp.zeros_like(s_ref)

    ps = pa_ref[0]

    r = ps[:, 0:BRANCH_W]
    k = ps[:, BRANCH_W:2 * BRANCH_W]
    v = ps[:, 2 * BRANCH_W:3 * BRANCH_W]
    gate = _silu(ps[:, 3 * BRANCH_W:4 * BRANCH_W])
    z = ps[:, 4 * BRANCH_W:4 * BRANCH_W + 2 * RW_RANK]
    zlane = lax.broadcasted_iota(jnp.int32, z.shape, 1)
    zz = jnp.where(zlane < RW_RANK, jnp.tanh(z), z).astype(BF16)
    proj = _dot(zz, w2a2_ref[...])
    ld = -_sigmoid(w0_ref[...] + proj[:, :BRANCH_W]) * math.exp(-0.5)
    a = _sigmoid(a0_ref[...] + proj[:, BRANCH_W:])
    kkv = k * kk_ref[...]
    k2 = k * (1.0 + (a - 1.0) * ka_ref[...])
    rkr = r * k2 * rk_ref[...]

    ti = lax.broadcasted_iota(jnp.int32, (t, t), 0)
    tj = lax.broadcasted_iota(jnp.int32, (t, t), 1)
    same_chunk = (ti // c) == (tj // c)
    tril = jnp.where(jnp.logical_and(same_chunk, ti >= tj), 1.0, 0.0).astype(BF16)
    ld_hi = ld.astype(BF16)
    ld_lo = (ld - ld_hi.astype(F32)).astype(BF16)
    cum = _dot(tril, ld_hi) + _dot(tril, ld_lo)
    e_cum = jnp.exp(cum)
    e_ncum = 1.0 / e_cum
    e_cumx = jnp.exp(cum - ld)

    ci = lax.broadcasted_iota(jnp.int32, (c, pw), 0)
    lane = lax.broadcasted_iota(jnp.int32, (c, pw), 1)
    cj = lane & (c - 1)
    lo = lane < n
    strict = ci > cj
    incl = ci >= cj
    eye = jnp.where(ci == cj, 1.0, 0.0)
    lo_b = jnp.where(lo, 1.0, 0.0).astype(BF16)
    hi_b = jnp.where(lo, 0.0, 1.0).astype(BF16)
    lvl_masks = []
    shift = 0
    while (1 << shift) < c:
        same_big = (ci >> (shift + 1)) == (cj >> (shift + 1))
        diff_small = (ci >> shift) != (cj >> shift)
        lvl_masks.append(jnp.logical_and(same_big, diff_small))
        shift += 1
    bi = lax.broadcasted_iota(jnp.int32, (pw, pw), 0)
    bj = lax.broadcasted_iota(jnp.int32, (pw, pw), 1)
    blockdiag = (bi >= n) == (bj >= n)
    br, bc = bi & (c - 1), bj & (c - 1)
    lvl_bd = []
    for shift in range(1, len(lvl_masks)):
        same_big = (br >> (shift + 1)) == (bc >> (shift + 1))
        diff_small = (br >> shift) != (bc >> shift)
        keep = jnp.logical_and(blockdiag, jnp.logical_and(same_big, diff_small))
        lvl_bd.append(jnp.where(keep, 1.0, 0.0).astype(BF16))

    def bd(y):
        return jnp.concatenate([y * lo_b, y * hi_b], axis=0)

    def half_sums(x):
        s_lo = jnp.sum(jnp.where(lo, x, 0.0), axis=-1, keepdims=True)
        s_hi = jnp.sum(jnp.where(lo, 0.0, x), axis=-1, keepdims=True)
        return jnp.where(lo, s_lo, s_hi)

    pairs = range(RW_HEADS // 2)
    sls = [slice(j * pw, (j + 1) * pw) for j in pairs]
    state = [s_ref[j] for j in pairs]
    outs = []
    for q in range(nchunks):
        rows = slice(q * c, (q + 1) * c)
        blk = lambda arr, sl: arr[rows, sl]
        g_c = jnp.exp(cum[(q + 1) * c - 1:(q + 1) * c, :])
        kh = [blk(kkv, sl) * jnp.minimum(lax.rsqrt(half_sums(blk(kkv, sl) * blk(kkv, sl))), 1e12)
              for sl in sls]
        b2 = [kh[j] * blk(a, sls[j]) for j in pairs]
        e_tot = [blk(e_ncum, sl) * g_c[:, sl] for sl in sls]
        kt = [(kh[j] * blk(e_cumx, sls[j])).astype(BF16) for j in pairs]
        bt = [(b2[j] * blk(e_ncum, sls[j])).astype(BF16) for j in pairs]
        kkt = [(blk(k2, sl) * blk(e_ncum, sl)).astype(BF16) for sl in sls]
        rt = [(blk(r, sl) * blk(e_cum, sl)).astype(BF16) for sl in sls]
        be_n = [(-(b2[j] * e_tot[j])).astype(BF16) for j in pairs]
        ke = [(blk(k2, sls[j]) * e_tot[j]).astype(BF16) for j in pairs]
        vb = [blk(v, sl).astype(BF16) for sl in sls]

        x2 = [jnp.concatenate([kt[j], rt[j]], axis=0) for j in pairs]
        rr = [jnp.concatenate([bt[j] * lo_b, bt[j] * hi_b, kkt[j] * lo_b, kkt[j] * hi_b], axis=0)
              for j in pairs]
        gall = [_dot_nt(x2[j], rr[j]) for j in pairs]
        l_m = [jnp.where(strict, gall[j][:c, :pw], 0.0) for j in pairs]
        akk = [jnp.where(strict, gall[j][:c, pw:], 0.0).astype(BF16) for j in pairs]
        arb_n = [jnp.where(incl, -gall[j][c:, :pw], 0.0).astype(BF16) for j in pairs]
        ark = [jnp.where(incl, gall[j][c:, pw:], 0.0).astype(BF16) for j in pairs]

        minv = [eye - jnp.where(lvl_masks[0], l_m[j], 0.0) for j in pairs]
        l_b = [l_m[j].astype(BF16) for j in pairs]
        l_t = [jnp.concatenate([l_b[j], l_b[j]], axis=0) for j in pairs]
        for m_bd in lvl_bd:
            mb = [minv[j].astype(BF16) for j in pairs]
            t1 = [_dot(mb[j], l_t[j] * m_bd).astype(BF16) for j in pairs]
            minv = [minv[j] - _dot(t1[j], bd(mb[j])) for j in pairs]
        mb = [minv[j].astype(BF16) for j in pairs]

        bdv = [bd(vb[j]) for j in pairs]
        av = [_dot(akk[j], bdv[j]) for j in pairs]
        s_b = [state[j].astype(BF16) for j in pairs]
        ks = [_dot_nt(x2[j], s_b[j]) for j in pairs]
        u_b = [_dot(mb[j], bd((ks[j][:c] + av[j]).astype(BF16))).astype(BF16) for j in pairs]
        y = [ks[j][c:] + _dot(jnp.concatenate([ark[j], arb_n[j]], axis=1),
                              jnp.concatenate([bdv[j], bd(u_b[j])], axis=0)) for j in pairs]
        state = [state[j] * g_c[:, sls[j]]
                 + jnp.where(blockdiag,
                             _dot_tn(jnp.concatenate([vb[j], u_b[j]], axis=0),
                                     jnp.concatenate([ke[j], be_n[j]], axis=0)), 0.0)
                 for j in pairs]

        for j in pairs:
            sl = sls[j]
            yc = y[j] - half_sums(y[j]) * (1.0 / n)
            var = half_sums(yc * yc) * (1.0 / n)
            yn = yc * lax.rsqrt(var + RW_LN_EPS) * lnw_ref[:, sl] + lnb_ref[:, sl]
            bonus = half_sums(blk(rkr, sl)) * blk(v, sl)
            outs.append(((yn + bonus) * blk(gate, sl)).astype(o_ref.dtype))
    for j in pairs:
        s_ref[j] = state[j]
    for q in range(nchunks):
        for j in pairs:
            o_ref[0, q * c:(q + 1) * c, sls[j]] = outs[q * len(sls) + j]


def _rwkv_branch(pa, w0, w2a2, a0, kk, ka, rk, lnw, lnb):
    bsz, seq, cols = pa.shape
    row = lambda t: t.reshape(1, -1)
    vec = lambda width: pl.BlockSpec((1, width), lambda b, c: (0, 0))
    blocks = RW_STEP * cols * 4 + 8 * BRANCH_W * 4 + w2a2.size * 2 + RW_STEP * BRANCH_W * 2
    scratch = (RW_HEADS // 2) * LANES * LANES * 4
    return pl.pallas_call(
        _rwkv_kernel,
        grid=(bsz, seq // RW_STEP),
        in_specs=[pl.BlockSpec((1, RW_STEP, cols), lambda b, c: (b, c, 0)),
                  vec(BRANCH_W),
                  pl.BlockSpec((2 * RW_RANK, 2 * BRANCH_W), lambda b, c: (0, 0)),
                  vec(BRANCH_W), vec(BRANCH_W), vec(BRANCH_W), vec(BRANCH_W),
                  vec(BRANCH_W), vec(BRANCH_W)],
        out_specs=pl.BlockSpec((1, RW_STEP, BRANCH_W), lambda b, c: (b, c, 0)),
        out_shape=jax.ShapeDtypeStruct((bsz, seq, BRANCH_W), BF16),
        scratch_shapes=[pltpu.VMEM((RW_HEADS // 2, LANES, LANES), F32)],
        compiler_params=_cparams(("parallel", "arbitrary"), blocks, scratch),
    )(pa, row(w0), w2a2, row(a0), row(kk), row(ka), row(rk), row(lnw), row(lnb))


def _ssd_kernel(pb_ref, cw_ref, cb_ref, dtb_ref, alog_ref, d_ref, nw_ref, e64_ref, e128_ref,
                o_ref, ext_ref, st_ref):
    l = SSD_CHUNK
    gw = BRANCH_W // SSD_GROUPS
    hpg = SSD_HEADS // SSD_GROUPS

    @pl.when(pl.program_id(1) == 0)
    def _():
        ext_ref[0:HALO, :] = jnp.zeros((HALO, SSD_XBC), F32)
        st_ref[...] = jnp.zeros_like(st_ref)

    pb = pb_ref[0]
    zgate = _silu(pb[:, 0:BRANCH_W])
    xbc = pb[:, BRANCH_W:BRANCH_W + SSD_XBC]
    dts = pb[:, BRANCH_W + SSD_XBC:BRANCH_W + SSD_XBC + LANES]

    ext_ref[HALO:HALO + l, :] = xbc
    conv = cb_ref[...] + cw_ref[SSD_CONV - 1:SSD_CONV, :] * xbc
    for kk in range(SSD_CONV - 1):
        back = SSD_CONV - 1 - kk
        conv = conv + cw_ref[kk:kk + 1, :] * ext_ref[pl.ds(HALO - back, l), :]
    ext_ref[0:HALO, :] = xbc[l - HALO:l, :]
    xa = _silu(conv)
    xs = xa[:, 0:BRANCH_W]
    bm = xa[:, BRANCH_W:BRANCH_W + SSD_GROUPS * SSD_STATE]
    cm = xa[:, BRANCH_W + SSD_GROUPS * SSD_STATE:]

    dt16 = _softplus(dts + dtb_ref[...])
    da16 = dt16 * (-jnp.exp(alog_ref[...]))
    ri = lax.broadcasted_iota(jnp.int32, (l, l), 0)
    rj = lax.broadcasted_iota(jnp.int32, (l, l), 1)
    causal = ri >= rj
    tril = jnp.where(causal, 1.0, 0.0).astype(BF16)
    acs16 = _dot3_left(tril, da16)
    acs_t = acs16.T
    e64 = e64_ref[...]
    dt_full = _dot3(dt16, e64)
    acs_full = _dot3(acs16, e64)
    acs_b = _dot3(acs16, e128_ref[...])
    tot_full = acs_full[l - 1:l, :]
    dte = jnp.exp(tot_full - acs_full)
    eacs = jnp.exp(acs_full)
    dec = jnp.exp(tot_full)
    xdt = xs * dt_full
    xdte = (xdt * dte).astype(BF16)
    glane = lax.broadcasted_iota(jnp.int32, (l, gw), 1) // SSD_HEAD

    outs = []
    st_new = []
    for g in range(SSD_GROUPS):
        gs = slice(g * gw, (g + 1) * gw)
        c_g = cm[:, g * SSD_STATE:(g + 1) * SSD_STATE].astype(BF16)
        b_g = bm[:, g * SSD_STATE:(g + 1) * SSD_STATE].astype(BF16)
        scores = _dot_nt(c_g, b_g)
        xdt_g = xdt[:, gs]
        p_parts = []
        x_parts = []
        for jj in range(hpg):
            j = g * hpg + jj
            diff = acs_b[:, j * LANES:(j + 1) * LANES] - acs_t[j:j + 1, :]
            seg = jnp.exp(jnp.where(causal, diff, -jnp.inf))
            p_parts.append((scores * seg).astype(BF16))
            x_parts.append(jnp.where(glane == jj, xdt_g, 0.0).astype(BF16))
        y_diag = _dot(jnp.concatenate(p_parts, axis=1), jnp.concatenate(x_parts, axis=0))
        st = st_ref[g]
        y_off = _dot(c_g, st.astype(BF16)) * eacs[:, gs]
        st_new.append(st * dec[:, gs] + _dot_tn(b_g, xdte[:, gs]))
        y = (y_diag + y_off + d_ref[:, gs] * xs[:, gs]) * zgate[:, gs]
        ms = jnp.mean(y * y, axis=-1, keepdims=True)
        outs.append((y * lax.rsqrt(ms + SSD_NORM_EPS) * nw_ref[:, gs]).astype(o_ref.dtype))
    for g in range(SSD_GROUPS):
        st_ref[g] = st_new[g]
    for g in range(SSD_GROUPS):
        o_ref[0, :, g * gw:(g + 1) * gw] = outs[g]


def _ssd_branch(pb, conv_w, conv_b, dt_bias, a_log, d_skip, norm_w):
    bsz, seq, cols = pb.shape
    pad16 = lambda t: jnp.pad(t, (0, LANES - SSD_HEADS)).reshape(1, LANES)
    head_of_lane64 = jnp.arange(BRANCH_W) // SSD_HEAD
    head_of_lane128 = jnp.arange(SSD_HEADS * LANES) // LANES
    e64 = (jnp.arange(LANES)[:, None] == head_of_lane64[None, :]).astype(BF16)
    e128 = (jnp.arange(LANES)[:, None] == head_of_lane128[None, :]).astype(BF16)
    d_full = jnp.repeat(d_skip, SSD_HEAD).reshape(1, BRANCH_W)
    const = lambda shape: pl.BlockSpec(shape, lambda b, c: (0, 0))
    blocks = (SSD_CHUNK * cols * 4 + (SSD_CONV + 1) * SSD_XBC * 4 + 4 * BRANCH_W * 4
              + e64.size * 2 + e128.size * 2 + SSD_CHUNK * BRANCH_W * 2)
    scratch = (HALO + SSD_CHUNK) * SSD_XBC * 4 + SSD_STATE * BRANCH_W * 4
    return pl.pallas_call(
        _ssd_kernel,
        grid=(bsz, seq // SSD_CHUNK),
        in_specs=[pl.BlockSpec((1, SSD_CHUNK, cols), lambda b, c: (b, c, 0)),
                  const((SSD_CONV, SSD_XBC)), const((1, SSD_XBC)), const((1, LANES)),
                  const((1, LANES)), const((1, BRANCH_W)), const((1, BRANCH_W)),
                  const((LANES, BRANCH_W)), const((LANES, SSD_HEADS * LANES))],
        out_specs=pl.BlockSpec((1, SSD_CHUNK, BRANCH_W), lambda b, c: (b, c, 0)),
        out_shape=jax.ShapeDtypeStruct((bsz, seq, BRANCH_W), BF16),
        scratch_shapes=[pltpu.VMEM((HALO + SSD_CHUNK, SSD_XBC), F32),
                        pltpu.VMEM((SSD_GROUPS, SSD_STATE, BRANCH_W // SSD_GROUPS), F32)],
        compiler_params=_cparams(("parallel", "arbitrary"), blocks, scratch),
    )(pb, conv_w, conv_b.reshape(1, -1), pad16(dt_bias), pad16(a_log), d_full,
      norm_w.reshape(1, -1), e64, e128)


def _sconv_kernel(pc_ref, cw_ref, o_ref, ext_ref):
    t = pc_ref.shape[1]

    @pl.when(pl.program_id(1) == 0)
    def _():
        ext_ref[0:HALO, :] = jnp.zeros((HALO, BRANCH_W), F32)

    pc = pc_ref[0].astype(F32)
    bg = pc[:, 0:BRANCH_W]
    u = pc[:, BRANCH_W:2 * BRANCH_W] * pc[:, 2 * BRANCH_W:3 * BRANCH_W]
    gate = _silu(pc[:, 3 * BRANCH_W:4 * BRANCH_W])
    ext_ref[HALO:HALO + t, :] = u
    conv = cw_ref[SC_CONV - 1:SC_CONV, :] * u
    for kk in range(SC_CONV - 1):
        back = SC_CONV - 1 - kk
        conv = conv + cw_ref[kk:kk + 1, :] * ext_ref[pl.ds(HALO - back, t), :]
    ext_ref[0:HALO, :] = u[t - HALO:t, :]
    o_ref[0] = (bg * conv * gate).astype(o_ref.dtype)


def _sconv_branch(pc, conv_w, tm=512):
    bsz, seq, cols = pc.shape
    blocks = tm * cols * pc.dtype.itemsize + SC_CONV * BRANCH_W * 4 + tm * BRANCH_W * 2
    scratch = (HALO + tm) * BRANCH_W * 4
    return pl.pallas_call(
        _sconv_kernel,
        grid=(bsz, seq // tm),
        in_specs=[pl.BlockSpec((1, tm, cols), lambda b, c: (b, c, 0)),
                  pl.BlockSpec((SC_CONV, BRANCH_W), lambda b, c: (0, 0))],
        out_specs=pl.BlockSpec((1, tm, BRANCH_W), lambda b, c: (b, c, 0)),
        out_shape=jax.ShapeDtypeStruct((bsz, seq, BRANCH_W), BF16),
        scratch_shapes=[pltpu.VMEM((HALO + tm, BRANCH_W), F32)],
        compiler_params=_cparams(("parallel", "arbitrary"), blocks, scratch),
    )(pc, conv_w)


def _merge_out_kernel(oa_ref, ob_ref, oc_ref, ga_ref, gb_ref, gc_ref, wb_ref, wo_ref, x_ref,
                      nw_ref, *o_refs):
    acc = _sigmoid(ga_ref[...].astype(F32)) * _dot(oa_ref[...], wb_ref[0])
    acc = acc + _sigmoid(gb_ref[...].astype(F32)) * _dot(ob_ref[...], wb_ref[1])
    acc = acc + _sigmoid(gc_ref[...].astype(F32)) * _dot(oc_ref[...], wb_ref[2])
    xn = x_ref[...] + _dot(acc.astype(BF16), wo_ref[...])
    ms = jnp.mean(xn * xn, axis=-1, keepdims=True)
    h_ref = o_refs[-1]
    h_ref[...] = (xn * lax.rsqrt(ms + NORM_EPS) * nw_ref[...]).astype(h_ref.dtype)
    if len(o_refs) == 2:
        o_refs[0][...] = xn


def _merge_out(oa, ob, oc, pg, wb, wo, x, nw, norm_dtype, keep_residual, tm=256):
    m = oa.shape[0]
    d = D_MODEL
    row = lambda i: (i, 0)
    once = pl.Buffered(1)
    o_spec = pl.BlockSpec((tm, BRANCH_W), row)
    g_spec = lambda nb: pl.BlockSpec((tm, d), lambda i: (i, nb))
    norm_bytes = jnp.dtype(norm_dtype).itemsize
    blocks = (N_BRANCH * tm * BRANCH_W * 2 + N_BRANCH * tm * d * pg.dtype.itemsize
              + tm * d * (4 + norm_bytes) + d * 4)
    out_specs = [pl.BlockSpec((tm, d), row)]
    out_shape = [jax.ShapeDtypeStruct((m, d), norm_dtype)]
    if keep_residual:
        blocks += tm * d * 4
        out_specs.insert(0, pl.BlockSpec((tm, d), row))
        out_shape.insert(0, jax.ShapeDtypeStruct((m, d), F32))
    resident = wb.size * 2 + wo.size * 2 + tm * d * 6
    return pl.pallas_call(
        _merge_out_kernel,
        grid=(m // tm,),
        in_specs=[o_spec, o_spec, o_spec, g_spec(0), g_spec(1), g_spec(2),
                  pl.BlockSpec((N_BRANCH, BRANCH_W, d), lambda i: (0, 0, 0), pipeline_mode=once),
                  pl.BlockSpec((d, d), lambda i: (0, 0), pipeline_mode=once),
                  pl.BlockSpec((tm, d), row),
                  pl.BlockSpec((1, d), lambda i: (0, 0))],
        out_specs=out_specs,
        out_shape=out_shape,
        compiler_params=_cparams(("parallel",), blocks, resident),
    )(oa, ob, oc, pg, pg, pg, wb, wo, x, nw.reshape(1, d))


def _layer(x2, h, bsz, seq, next_norm_w, last, w_in, rw_mu, rw_w0, rw_w2, rw_a0, rw_a2, rw_kk,
           rw_ka, rw_rk, rw_ln_w, rw_ln_b, ssd_conv_w, ssd_conv_b, ssd_dt_bias, ssd_a_log, ssd_d,
           ssd_norm_w, sc_conv_w, w_branch, w_out):
    m = bsz * seq
    w_a, w_b, w_c, w_g = _split_w_in(w_in)
    zeros = jnp.zeros((RW_RANK, BRANCH_W), F32)
    w2a2 = jnp.concatenate([jnp.concatenate([rw_w2, zeros], axis=1),
                            jnp.concatenate([zeros, rw_a2], axis=1)], axis=0).astype(BF16)

    pa = _matmul_shift(h, w_a, rw_mu, seq, tm=1024, tn=1408)
    pb = _matmul(h, w_b, F32, tm=512, tn=SSD_COLS_PAD)
    pc = _matmul(h, w_c, BF16, tm=1024, tn=2048)
    pg = _matmul(h, w_g, BF16, tm=1024, tn=2048)

    oa = _rwkv_branch(pa.reshape(bsz, seq, RW_COLS), rw_w0, w2a2, rw_a0, rw_kk, rw_ka,
                      rw_rk.reshape(-1), rw_ln_w, rw_ln_b)
    ob = _ssd_branch(pb.reshape(bsz, seq, SSD_COLS_PAD), ssd_conv_w, ssd_conv_b, ssd_dt_bias,
                     ssd_a_log, ssd_d, ssd_norm_w)
    oc = _sconv_branch(pc.reshape(bsz, seq, SC_COLS), sc_conv_w)

    wb = _cast_bf16(w_branch.reshape(N_BRANCH * BRANCH_W, D_MODEL)).reshape(N_BRANCH, BRANCH_W, D_MODEL)
    outs = _merge_out(oa.reshape(m, BRANCH_W), ob.reshape(m, BRANCH_W), oc.reshape(m, BRANCH_W), pg,
                      wb, _cast_bf16(w_out), x2, next_norm_w, F32 if last else BF16,
                      keep_residual=not last)
    if last:
        return None, outs[0]
    return outs


def kernel(x, norm_w, w_in, rw_mu, rw_w0, rw_w2, rw_a0, rw_a2, rw_kk, rw_ka, rw_rk, rw_ln_w,
           rw_ln_b, ssd_conv_w, ssd_conv_b, ssd_dt_bias, ssd_a_log, ssd_d, ssd_norm_w,
           sc_conv_w, w_branch, w_out, final_norm_w):
    bsz, seq, d = x.shape
    x2 = x.reshape(bsz * seq, d)
    h = _rmsnorm(x2, norm_w[0], BF16)
    for l in range(DEPTH):
        last = l == DEPTH - 1
        next_norm_w = final_norm_w if last else norm_w[l + 1]
        x2, h = _layer(x2, h, bsz, seq, next_norm_w, last, w_in[l], rw_mu[l], rw_w0[l], rw_w2[l],
                       rw_a0[l], rw_a2[l], rw_kk[l], rw_ka[l], rw_rk[l], rw_ln_w[l], rw_ln_b[l],
                       ssd_conv_w[l], ssd_conv_b[l], ssd_dt_bias[l], ssd_a_log[l], ssd_d[l],
                       ssd_norm_w[l], sc_conv_w[l], w_branch[l], w_out[l])
    return h.reshape(bsz, seq, d)
```

```python
import functools
import math

import jax
import jax.numpy as jnp
from jax import lax
from jax.experimental import pallas as pl
from jax.experimental.pallas import tpu as pltpu

F32 = jnp.float32
BF16 = jnp.bfloat16

D_MODEL = 2048
DEPTH = 2
BRANCH_W = D_MODEL // 2
N_BRANCH = 3
RW_HEAD = 64
RW_HEADS = BRANCH_W // RW_HEAD
RW_RANK = 64
RW_LN_EPS = 64e-5
RW_COLS = 4 * BRANCH_W + 2 * RW_RANK
SSD_HEAD = 64
SSD_HEADS = BRANCH_W // SSD_HEAD
SSD_GROUPS = 4
SSD_STATE = 128
SSD_CONV = 4
SSD_XBC = BRANCH_W + 2 * SSD_GROUPS * SSD_STATE
SSD_COLS = BRANCH_W + SSD_XBC + SSD_HEADS
SSD_NORM_EPS = 1e-5
SC_CONV = 3
SC_COLS = 4 * BRANCH_W
GATE_COLS = N_BRANCH * D_MODEL
NORM_EPS = 1e-6

LANES = 128
SSD_COLS_PAD = 3200
RW_CHUNK = 64
RW_STEP = 256
SSD_CHUNK = 128
HALO = 8
VMEM_PHYSICAL = 64 * 1024 * 1024
VMEM_INTERNAL = 8 * 1024 * 1024

assert RW_CHUNK == RW_HEAD and 2 * RW_HEAD == LANES


def _cparams(sem, block_bytes, scratch_bytes=0):
    limit = min(2 * block_bytes + scratch_bytes + VMEM_INTERNAL, VMEM_PHYSICAL - VMEM_INTERNAL)
    return pltpu.CompilerParams(dimension_semantics=sem, vmem_limit_bytes=limit)


def _sigmoid(x):
    return 1.0 / (1.0 + jnp.exp(-x))


def _silu(x):
    return x * _sigmoid(x)


def _softplus(x):
    return jnp.maximum(x, 0.0) + jnp.log1p(jnp.exp(-jnp.abs(x)))


def _dot(a, b):
    return jnp.dot(a, b, preferred_element_type=F32)


def _dot_nt(a, b):
    return lax.dot_general(a, b, (((1,), (1,)), ((), ())), preferred_element_type=F32)


def _dot_tn(a, b):
    return lax.dot_general(a, b, (((0,), (0,)), ((), ())), preferred_element_type=F32)


def _split3(x):
    hi = x.astype(BF16)
    r1 = x - hi.astype(F32)
    mid = r1.astype(BF16)
    lo = (r1 - mid.astype(F32)).astype(BF16)
    return hi, mid, lo


def _dot3(x, m):
    hi, mid, lo = _split3(x)
    return _dot(hi, m) + _dot(mid, m) + _dot(lo, m)


def _dot3_left(m, x):
    hi, mid, lo = _split3(x)
    return _dot(m, hi) + _dot(m, mid) + _dot(m, lo)


def _rmsnorm_kernel(x_ref, w_ref, o_ref):
    x = x_ref[...]
    ms = jnp.mean(x * x, axis=-1, keepdims=True)
    o_ref[...] = (x * lax.rsqrt(ms + NORM_EPS) * w_ref[...]).astype(o_ref.dtype)


def _rmsnorm(x, w, out_dtype, tm=512):
    m, d = x.shape
    blocks = tm * d * (4 + jnp.dtype(out_dtype).itemsize) + d * 4
    return pl.pallas_call(
        _rmsnorm_kernel,
        grid=(m // tm,),
        in_specs=[pl.BlockSpec((tm, d), lambda i: (i, 0)),
                  pl.BlockSpec((1, d), lambda i: (0, 0))],
        out_specs=pl.BlockSpec((tm, d), lambda i: (i, 0)),
        out_shape=jax.ShapeDtypeStruct((m, d), out_dtype),
        compiler_params=_cparams(("parallel",), blocks),
    )(x, w.reshape(1, d))


def _matmul_kernel(a_ref, b_ref, o_ref):
    o_ref[...] = _dot(a_ref[...], b_ref[...]).astype(o_ref.dtype)


def _matmul(a, b, out_dtype, tm, tn):
    m, k = a.shape
    _, n = b.shape
    blocks = tm * k * 2 + k * tn * 2 + tm * tn * jnp.dtype(out_dtype).itemsize
    return pl.pallas_call(
        _matmul_kernel,
        grid=(m // tm, n // tn),
        in_specs=[pl.BlockSpec((tm, k), lambda i, j: (i, 0)),
                  pl.BlockSpec((k, tn), lambda i, j: (0, j))],
        out_specs=pl.BlockSpec((tm, tn), lambda i, j: (i, j)),
        out_shape=jax.ShapeDtypeStruct((m, n), out_dtype),
        compiler_params=_cparams(("parallel", "parallel"), blocks),
    )(a, b)


def _matmul_shift_kernel(a_ref, b_ref, mu_ref, o_ref, carry_ref, *, tiles_per_seq):
    i = pl.program_id(0)
    j = pl.program_id(1)
    p = _dot(a_ref[...], b_ref[...])
    tm = p.shape[0]
    prev_last = jnp.where(lax.rem(i, tiles_per_seq) == 0, 0.0, carry_ref[j, 0:1, :])
    row = lax.broadcasted_iota(jnp.int32, p.shape, 0)
    shifted = jnp.where(row == 0, prev_last, pltpu.roll(p, 1, axis=0))
    carry_ref[j, 0:1, :] = p[tm - 1:tm, :]
    o_ref[...] = p + mu_ref[...] * (shifted - p)


def _matmul_shift(a, b, mu, layer, seq, tm, tn):
    m, k = a.shape
    _, n = b.shape
    assert seq % tm == 0
    blocks = tm * k * 2 + k * tn * 2 + tm * tn * 4 + tn * 4
    scratch = (n // tn) * HALO * tn * 4
    return pl.pallas_call(
        functools.partial(_matmul_shift_kernel, tiles_per_seq=seq // tm),
        grid=(m // tm, n // tn),
        in_specs=[pl.BlockSpec((tm, k), lambda i, j: (i, 0)),
                  pl.BlockSpec((k, tn), lambda i, j: (0, j)),
                  pl.BlockSpec((None, 1, tn), lambda i, j: (layer, 0, j))],
        out_specs=pl.BlockSpec((tm, tn), lambda i, j: (i, j)),
        out_shape=jax.ShapeDtypeStruct((m, n), F32),
        scratch_shapes=[pltpu.VMEM((n // tn, HALO, tn), F32)],
        compiler_params=_cparams(("arbitrary", "arbitrary"), blocks, scratch),
    )(a, b, _rows(mu))


def _split_w_in_kernel(w_ref, a_ref, b_ref, c_ref, g_ref):
    c0, c1, c2 = RW_COLS, RW_COLS + SSD_COLS, RW_COLS + SSD_COLS + SC_COLS
    a_ref[...] = w_ref[:, 0:c0].astype(BF16)
    b_ref[:, SSD_COLS_PAD - LANES:] = jnp.zeros((b_ref.shape[0], LANES), BF16)
    b_ref[:, 0:SSD_COLS] = w_ref[:, c0:c1].astype(BF16)
    c_ref[...] = w_ref[:, c1:c2].astype(BF16)
    g_ref[...] = w_ref[:, c2:].astype(BF16)


def _split_w_in(w_in, layer, tr=128):
    _, k, n = w_in.shape
    widths = (RW_COLS, SSD_COLS_PAD, SC_COLS, GATE_COLS)
    blocks = tr * n * 4 + tr * sum(widths) * 2
    return pl.pallas_call(
        _split_w_in_kernel,
        grid=(k // tr,),
        in_specs=[pl.BlockSpec((None, tr, n), lambda i: (layer, i, 0))],
        out_specs=[pl.BlockSpec((tr, w), lambda i: (i, 0)) for w in widths],
        out_shape=[jax.ShapeDtypeStruct((k, w), BF16) for w in widths],
        compiler_params=_cparams(("parallel",), blocks),
    )(w_in)


def _cast_kernel(x_ref, o_ref):
    o_ref[...] = x_ref[...].astype(o_ref.dtype)


def _cast_bf16(x, layer, tr=512):
    _, m, n = x.shape
    return pl.pallas_call(
        _cast_kernel,
        grid=(m // tr,),
        in_specs=[pl.BlockSpec((None, tr, n), lambda i: (layer, i, 0))],
        out_specs=pl.BlockSpec((tr, n), lambda i: (i, 0)),
        out_shape=jax.ShapeDtypeStruct((m, n), BF16),
        compiler_params=_cparams(("parallel",), tr * n * 6),
    )(x)


def _rwkv_kernel(pa_ref, w0_ref, w2a2_ref, a0_ref, kk_ref, ka_ref, rk_ref,
                 lnw_ref, lnb_ref, o_ref, s_ref):
    c = RW_CHUNK
    n = RW_HEAD
    pw = 2 * n
    t = RW_STEP
    nchunks = t // c

    @pl.when(pl.program_id(1) == 0)
    def _():
        s_ref[...] = jnp.zeros_like(s_ref)

    ps = pa_ref[0]

    r = ps[:, 0:BRANCH_W]
    k = ps[:, BRANCH_W:2 * BRANCH_W]
    v = ps[:, 2 * BRANCH_W:3 * BRANCH_W]
    gate = _silu(ps[:, 3 * BRANCH_W:4 * BRANCH_W])
    z = ps[:, 4 * BRANCH_W:4 * BRANCH_W + 2 * RW_RANK]
    zlane = lax.broadcasted_iota(jnp.int32, z.shape, 1)
    zz = jnp.where(zlane < RW_RANK, jnp.tanh(z), z).astype(BF16)
    proj = _dot(zz, w2a2_ref[...])
    ld = -_sigmoid(w0_ref[...] + proj[:, :BRANCH_W]) * math.exp(-0.5)
    a = _sigmoid(a0_ref[...] + proj[:, BRANCH_W:])
    kkv = k * kk_ref[...]
    k2 = k * (1.0 + (a - 1.0) * ka_ref[...])
    rkr = r * k2 * rk_ref[...]

    ti = lax.broadcasted_iota(jnp.int32, (t, t), 0)
    tj = lax.broadcasted_iota(jnp.int32, (t, t), 1)
    same_chunk = (ti // c) == (tj // c)
    tril = jnp.where(jnp.logical_and(same_chunk, ti >= tj), 1.0, 0.0).astype(BF16)
    ld_hi = ld.astype(BF16)
    ld_lo = (ld - ld_hi.astype(F32)).astype(BF16)
    cum = _dot(tril, ld_hi) + _dot(tril, ld_lo)
    e_cum = jnp.exp(cum)
    e_ncum = 1.0 / e_cum
    e_cumx = jnp.exp(cum - ld)

    ci = lax.broadcasted_iota(jnp.int32, (c, pw), 0)
    lane = lax.broadcasted_iota(jnp.int32, (c, pw), 1)
    cj = lane & (c - 1)
    lo = lane < n
    strict = ci > cj
    incl = ci >= cj
    eye = jnp.where(ci == cj, 1.0, 0.0)
    lo_b = jnp.where(lo, 1.0, 0.0).astype(BF16)
    hi_b = jnp.where(lo, 0.0, 1.0).astype(BF16)
    lvl_masks = []
    shift = 0
    while (1 << shift) < c:
        same_big = (ci >> (shift + 1)) == (cj >> (shift + 1))
        diff_small = (ci >> shift) != (cj >> shift)
        lvl_masks.append(jnp.logical_and(same_big, diff_small))
        shift += 1
    bi = lax.broadcasted_iota(jnp.int32, (pw, pw), 0)
    bj = lax.broadcasted_iota(jnp.int32, (pw, pw), 1)
    blockdiag = (bi >= n) == (bj >= n)
    br, bc = bi & (c - 1), bj & (c - 1)
    lvl_bd = []
    for shift in range(1, len(lvl_masks)):
        same_big = (br >> (shift + 1)) == (bc >> (shift + 1))
        diff_small = (br >> shift) != (bc >> shift)
        keep = jnp.logical_and(blockdiag, jnp.logical_and(same_big, diff_small))
        lvl_bd.append(jnp.where(keep, 1.0, 0.0).astype(BF16))

    def bd(y):
        return jnp.concatenate([y * lo_b, y * hi_b], axis=0)

    def half_sums(x):
        s_lo = jnp.sum(jnp.where(lo, x, 0.0), axis=-1, keepdims=True)
        s_hi = jnp.sum(jnp.where(lo, 0.0, x), axis=-1, keepdims=True)
        return jnp.where(lo, s_lo, s_hi)

    pairs = range(RW_HEADS // 2)
    sls = [slice(j * pw, (j + 1) * pw) for j in pairs]
    state = [s_ref[j] for j in pairs]
    outs = []
    for q in range(nchunks):
        rows = slice(q * c, (q + 1) * c)
        blk = lambda arr, sl: arr[rows, sl]
        g_c = jnp.exp(cum[(q + 1) * c - 1:(q + 1) * c, :])
        kh = [blk(kkv, sl) * jnp.minimum(lax.rsqrt(half_sums(blk(kkv, sl) * blk(kkv, sl))), 1e12)
              for sl in sls]
        b2 = [kh[j] * blk(a, sls[j]) for j in pairs]
        e_tot = [blk(e_ncum, sl) * g_c[:, sl] for sl in sls]
        kt = [(kh[j] * blk(e_cumx, sls[j])).astype(BF16) for j in pairs]
        bt = [(b2[j] * blk(e_ncum, sls[j])).astype(BF16) for j in pairs]
        kkt = [(blk(k2, sl) * blk(e_ncum, sl)).astype(BF16) for sl in sls]
        rt = [(blk(r, sl) * blk(e_cum, sl)).astype(BF16) for sl in sls]
        be_n = [(-(b2[j] * e_tot[j])).astype(BF16) for j in pairs]
        ke = [(blk(k2, sls[j]) * e_tot[j]).astype(BF16) for j in pairs]
        vb = [blk(v, sl).astype(BF16) for sl in sls]

        x2 = [jnp.concatenate([kt[j], rt[j]], axis=0) for j in pairs]
        rr = [jnp.concatenate([bt[j] * lo_b, bt[j] * hi_b, kkt[j] * lo_b, kkt[j] * hi_b], axis=0)
              for j in pairs]
        gall = [_dot_nt(x2[j], rr[j]) for j in pairs]
        l_m = [jnp.where(strict, gall[j][:c, :pw], 0.0) for j in pairs]
        akk = [jnp.where(strict, gall[j][:c, pw:], 0.0).astype(BF16) for j in pairs]
        arb_n = [jnp.where(incl, -gall[j][c:, :pw], 0.0).astype(BF16) for j in pairs]
        ark = [jnp.where(incl, gall[j][c:, pw:], 0.0).astype(BF16) for j in pairs]

        minv = [eye - jnp.where(lvl_masks[0], l_m[j], 0.0) for j in pairs]
        l_b = [l_m[j].astype(BF16) for j in pairs]
        l_t = [jnp.concatenate([l_b[j], l_b[j]], axis=0) for j in pairs]
        for m_bd in lvl_bd:
            mb = [minv[j].astype(BF16) for j in pairs]
            t1 = [_dot(mb[j], l_t[j] * m_bd).astype(BF16) for j in pairs]
            minv = [minv[j] - _dot(t1[j], bd(mb[j])) for j in pairs]
        mb = [minv[j].astype(BF16) for j in pairs]

        bdv = [bd(vb[j]) for j in pairs]
        av = [_dot(akk[j], bdv[j]) for j in pairs]
        s_b = [state[j].astype(BF16) for j in pairs]
        ks = [_dot_nt(x2[j], s_b[j]) for j in pairs]
        u_b = [_dot(mb[j], bd((ks[j][:c] + av[j]).astype(BF16))).astype(BF16) for j in pairs]
        y = [ks[j][c:] + _dot(jnp.concatenate([ark[j], arb_n[j]], axis=1),
                              jnp.concatenate([bdv[j], bd(u_b[j])], axis=0)) for j in pairs]
        state = [state[j] * g_c[:, sls[j]]
                 + jnp.where(blockdiag,
                             _dot_tn(jnp.concatenate([vb[j], u_b[j]], axis=0),
                                     jnp.concatenate([ke[j], be_n[j]], axis=0)), 0.0)
                 for j in pairs]

        for j in pairs:
            sl = sls[j]
            yc = y[j] - half_sums(y[j]) * (1.0 / n)
            var = half_sums(yc * yc) * (1.0 / n)
            yn = yc * lax.rsqrt(var + RW_LN_EPS) * lnw_ref[:, sl] + lnb_ref[:, sl]
            bonus = half_sums(blk(rkr, sl)) * blk(v, sl)
            outs.append(((yn + bonus) * blk(gate, sl)).astype(o_ref.dtype))
    for j in pairs:
        s_ref[j] = state[j]
    for q in range(nchunks):
        for j in pairs:
            o_ref[0, q * c:(q + 1) * c, sls[j]] = outs[q * len(sls) + j]


def _rows(p):
    return p.reshape(p.shape[0], 1, -1)


def _layer_spec(shape, layer):
    return pl.BlockSpec((None,) + tuple(shape), lambda *_: (layer,) + (0,) * len(shape))


def _rwkv_branch(pa, layer, w0, w2a2, a0, kk, ka, rk, lnw, lnb):
    bsz, seq, cols = pa.shape
    vec = _layer_spec((1, BRANCH_W), layer)
    blocks = RW_STEP * cols * 4 + 8 * BRANCH_W * 4 + w2a2[0].size * 2 + RW_STEP * BRANCH_W * 2
    scratch = (RW_HEADS // 2) * LANES * LANES * 4
    return pl.pallas_call(
        _rwkv_kernel,
        grid=(bsz, seq // RW_STEP),
        in_specs=[pl.BlockSpec((1, RW_STEP, cols), lambda b, c: (b, c, 0)),
                  vec, _layer_spec((2 * RW_RANK, 2 * BRANCH_W), layer),
                  vec, vec, vec, vec, vec, vec],
        out_specs=pl.BlockSpec((1, RW_STEP, BRANCH_W), lambda b, c: (b, c, 0)),
        out_shape=jax.ShapeDtypeStruct((bsz, seq, BRANCH_W), BF16),
        scratch_shapes=[pltpu.VMEM((RW_HEADS // 2, LANES, LANES), F32)],
        compiler_params=_cparams(("parallel", "arbitrary"), blocks, scratch),
    )(pa, _rows(w0), w2a2, _rows(a0), _rows(kk), _rows(ka), _rows(rk), _rows(lnw), _rows(lnb))


def _ssd_kernel(pb_ref, cw_ref, cb_ref, dtb_ref, alog_ref, d_ref, nw_ref, e64_ref, e128_ref,
                o_ref, ext_ref, st_ref):
    l = SSD_CHUNK
    gw = BRANCH_W // SSD_GROUPS
    hpg = SSD_HEADS // SSD_GROUPS

    @pl.when(pl.program_id(1) == 0)
    def _():
        ext_ref[0:HALO, :] = jnp.zeros((HALO, SSD_XBC), F32)
        st_ref[...] = jnp.zeros_like(st_ref)

    pb = pb_ref[0]
    zgate = _silu(pb[:, 0:BRANCH_W])
    xbc = pb[:, BRANCH_W:BRANCH_W + SSD_XBC]
    dts = pb[:, BRANCH_W + SSD_XBC:BRANCH_W + SSD_XBC + LANES]

    ext_ref[HALO:HALO + l, :] = xbc
    conv = cb_ref[...] + cw_ref[SSD_CONV - 1:SSD_CONV, :] * xbc
    for kk in range(SSD_CONV - 1):
        back = SSD_CONV - 1 - kk
        conv = conv + cw_ref[kk:kk + 1, :] * ext_ref[pl.ds(HALO - back, l), :]
    ext_ref[0:HALO, :] = xbc[l - HALO:l, :]
    xa = _silu(conv)
    xs = xa[:, 0:BRANCH_W]
    bm = xa[:, BRANCH_W:BRANCH_W + SSD_GROUPS * SSD_STATE]
    cm = xa[:, BRANCH_W + SSD_GROUPS * SSD_STATE:]

    dt16 = _softplus(dts + dtb_ref[...])
    da16 = dt16 * (-jnp.exp(alog_ref[...]))
    ri = lax.broadcasted_iota(jnp.int32, (l, l), 0)
    rj = lax.broadcasted_iota(jnp.int32, (l, l), 1)
    causal = ri >= rj
    tril = jnp.where(causal, 1.0, 0.0).astype(BF16)
    acs16 = _dot3_left(tril, da16)
    acs_t = acs16.T
    e64 = e64_ref[...]
    dt_full = _dot3(dt16, e64)
    acs_full = _dot3(acs16, e64)
    acs_b = _dot3(acs16, e128_ref[...])
    tot_full = acs_full[l - 1:l, :]
    dte = jnp.exp(tot_full - acs_full)
    eacs = jnp.exp(acs_full)
    dec = jnp.exp(tot_full)
    xdt = xs * dt_full
    xdte = (xdt * dte).astype(BF16)
    glane = lax.broadcasted_iota(jnp.int32, (l, gw), 1) // SSD_HEAD

    outs = []
    st_new = []
    for g in range(SSD_GROUPS):
        gs = slice(g * gw, (g + 1) * gw)
        c_g = cm[:, g * SSD_STATE:(g + 1) * SSD_STATE].astype(BF16)
        b_g = bm[:, g * SSD_STATE:(g + 1) * SSD_STATE].astype(BF16)
        scores = _dot_nt(c_g, b_g)
        xdt_g = xdt[:, gs]
        p_parts = []
        x_parts = []
        for jj in range(hpg):
            j = g * hpg + jj
            diff = acs_b[:, j * LANES:(j + 1) * LANES] - acs_t[j:j + 1, :]
            seg = jnp.exp(jnp.where(causal, diff, -jnp.inf))
            p_parts.append((scores * seg).astype(BF16))
            x_parts.append(jnp.where(glane == jj, xdt_g, 0.0).astype(BF16))
        y_diag = _dot(jnp.concatenate(p_parts, axis=1), jnp.concatenate(x_parts, axis=0))
        st = st_ref[g]
        y_off = _dot(c_g, st.astype(BF16)) * eacs[:, gs]
        st_new.append(st * dec[:, gs] + _dot_tn(b_g, xdte[:, gs]))
        y = (y_diag + y_off + d_ref[:, gs] * xs[:, gs]) * zgate[:, gs]
        ms = jnp.mean(y * y, axis=-1, keepdims=True)
        outs.append((y * lax.rsqrt(ms + SSD_NORM_EPS) * nw_ref[:, gs]).astype(o_ref.dtype))
    for g in range(SSD_GROUPS):
        st_ref[g] = st_new[g]
    for g in range(SSD_GROUPS):
        o_ref[0, :, g * gw:(g + 1) * gw] = outs[g]


def _ssd_head_expanders():
    head_of_lane64 = jnp.arange(BRANCH_W) // SSD_HEAD
    head_of_lane128 = jnp.arange(SSD_HEADS * LANES) // LANES
    e64 = (jnp.arange(LANES)[:, None] == head_of_lane64[None, :]).astype(BF16)
    e128 = (jnp.arange(LANES)[:, None] == head_of_lane128[None, :]).astype(BF16)
    return e64, e128


def _ssd_branch(pb, layer, conv_w, conv_b, dt_bias, a_log, d_skip, norm_w, e64, e128):
    bsz, seq, cols = pb.shape
    pad16 = lambda t: jnp.pad(t, ((0, 0), (0, LANES - SSD_HEADS)))
    d_full = jnp.repeat(d_skip, SSD_HEAD, axis=1)
    const = lambda shape: pl.BlockSpec(shape, lambda b, c: (0, 0))
    blocks = (SSD_CHUNK * cols * 4 + (SSD_CONV + 1) * SSD_XBC * 4 + 4 * BRANCH_W * 4
              + e64.size * 2 + e128.size * 2 + SSD_CHUNK * BRANCH_W * 2)
    scratch = (HALO + SSD_CHUNK) * SSD_XBC * 4 + SSD_STATE * BRANCH_W * 4
    return pl.pallas_call(
        _ssd_kernel,
        grid=(bsz, seq // SSD_CHUNK),
        in_specs=[pl.BlockSpec((1, SSD_CHUNK, cols), lambda b, c: (b, c, 0)),
                  _layer_spec((SSD_CONV, SSD_XBC), layer), _layer_spec((1, SSD_XBC), layer),
                  _layer_spec((1, LANES), layer), _layer_spec((1, LANES), layer),
                  _layer_spec((1, BRANCH_W), layer), _layer_spec((1, BRANCH_W), layer),
                  const((LANES, BRANCH_W)), const((LANES, SSD_HEADS * LANES))],
        out_specs=pl.BlockSpec((1, SSD_CHUNK, BRANCH_W), lambda b, c: (b, c, 0)),
        out_shape=jax.ShapeDtypeStruct((bsz, seq, BRANCH_W), BF16),
        scratch_shapes=[pltpu.VMEM((HALO + SSD_CHUNK, SSD_XBC), F32),
                        pltpu.VMEM((SSD_GROUPS, SSD_STATE, BRANCH_W // SSD_GROUPS), F32)],
        compiler_params=_cparams(("parallel", "arbitrary"), blocks, scratch),
    )(pb, conv_w, _rows(conv_b), _rows(pad16(dt_bias)), _rows(pad16(a_log)), _rows(d_full),
      _rows(norm_w), e64, e128)


def _sconv_kernel(pc_ref, cw_ref, o_ref, ext_ref):
    t = pc_ref.shape[1]

    @pl.when(pl.program_id(1) == 0)
    def _():
        ext_ref[0:HALO, :] = jnp.zeros((HALO, BRANCH_W), F32)

    pc = pc_ref[0].astype(F32)
    bg = pc[:, 0:BRANCH_W]
    u = pc[:, BRANCH_W:2 * BRANCH_W] * pc[:, 2 * BRANCH_W:3 * BRANCH_W]
    gate = _silu(pc[:, 3 * BRANCH_W:4 * BRANCH_W])
    ext_ref[HALO:HALO + t, :] = u
    conv = cw_ref[SC_CONV - 1:SC_CONV, :] * u
    for kk in range(SC_CONV - 1):
        back = SC_CONV - 1 - kk
        conv = conv + cw_ref[kk:kk + 1, :] * ext_ref[pl.ds(HALO - back, t), :]
    ext_ref[0:HALO, :] = u[t - HALO:t, :]
    o_ref[0] = (bg * conv * gate).astype(o_ref.dtype)


def _sconv_branch(pc, layer, conv_w, tm=512):
    bsz, seq, cols = pc.shape
    blocks = tm * cols * pc.dtype.itemsize + SC_CONV * BRANCH_W * 4 + tm * BRANCH_W * 2
    scratch = (HALO + tm) * BRANCH_W * 4
    return pl.pallas_call(
        _sconv_kernel,
        grid=(bsz, seq // tm),
        in_specs=[pl.BlockSpec((1, tm, cols), lambda b, c: (b, c, 0)),
                  _layer_spec((SC_CONV, BRANCH_W), layer)],
        out_specs=pl.BlockSpec((1, tm, BRANCH_W), lambda b, c: (b, c, 0)),
        out_shape=jax.ShapeDtypeStruct((bsz, seq, BRANCH_W), BF16),
        scratch_shapes=[pltpu.VMEM((HALO + tm, BRANCH_W), F32)],
        compiler_params=_cparams(("parallel", "arbitrary"), blocks, scratch),
    )(pc, conv_w)


def _merge_out_kernel(oa_ref, ob_ref, oc_ref, ga_ref, gb_ref, gc_ref, wb_ref, wo_ref, x_ref,
                      nw_ref, *o_refs):
    acc = _sigmoid(ga_ref[...].astype(F32)) * _dot(oa_ref[...], wb_ref[0])
    acc = acc + _sigmoid(gb_ref[...].astype(F32)) * _dot(ob_ref[...], wb_ref[1])
    acc = acc + _sigmoid(gc_ref[...].astype(F32)) * _dot(oc_ref[...], wb_ref[2])
    xn = x_ref[...] + _dot(acc.astype(BF16), wo_ref[...])
    ms = jnp.mean(xn * xn, axis=-1, keepdims=True)
    h_ref = o_refs[-1]
    h_ref[...] = (xn * lax.rsqrt(ms + NORM_EPS) * nw_ref[...]).astype(h_ref.dtype)
    if len(o_refs) == 2:
        o_refs[0][...] = xn


def _merge_out(oa, ob, oc, pg, wb, wo, x, nw, norm_dtype, keep_residual, tm=256):
    m = oa.shape[0]
    d = D_MODEL
    row = lambda i: (i, 0)
    once = pl.Buffered(1)
    o_spec = pl.BlockSpec((tm, BRANCH_W), row)
    g_spec = lambda nb: pl.BlockSpec((tm, d), lambda i: (i, nb))
    norm_bytes = jnp.dtype(norm_dtype).itemsize
    blocks = (N_BRANCH * tm * BRANCH_W * 2 + N_BRANCH * tm * d * pg.dtype.itemsize
              + tm * d * (4 + norm_bytes) + d * 4)
    out_specs = [pl.BlockSpec((tm, d), row)]
    out_shape = [jax.ShapeDtypeStruct((m, d), norm_dtype)]
    if keep_residual:
        blocks += tm * d * 4
        out_specs.insert(0, pl.BlockSpec((tm, d), row))
        out_shape.insert(0, jax.ShapeDtypeStruct((m, d), F32))
    resident = wb.size * 2 + wo.size * 2 + tm * d * 6
    return pl.pallas_call(
        _merge_out_kernel,
        grid=(m // tm,),
        in_specs=[o_spec, o_spec, o_spec, g_spec(0), g_spec(1), g_spec(2),
                  pl.BlockSpec((N_BRANCH, BRANCH_W, d), lambda i: (0, 0, 0), pipeline_mode=once),
                  pl.BlockSpec((d, d), lambda i: (0, 0), pipeline_mode=once),
                  pl.BlockSpec((tm, d), row),
                  pl.BlockSpec((1, d), lambda i: (0, 0))],
        out_specs=out_specs,
        out_shape=out_shape,
        compiler_params=_cparams(("parallel",), blocks, resident),
    )(oa, ob, oc, pg, pg, pg, wb, wo, x, nw.reshape(1, d))


def _layer(x2, h, bsz, seq, next_norm_w, last, layer, p):
    m = bsz * seq
    w_a, w_b, w_c, w_g = _split_w_in(p["w_in"], layer)

    pa = _matmul_shift(h, w_a, p["rw_mu"], layer, seq, tm=1024, tn=1408)
    pb = _matmul(h, w_b, F32, tm=512, tn=SSD_COLS_PAD)
    pc = _matmul(h, w_c, BF16, tm=1024, tn=2048)
    pg = _matmul(h, w_g, BF16, tm=1024, tn=2048)

    oa = _rwkv_branch(pa.reshape(bsz, seq, RW_COLS), layer, p["rw_w0"], p["rw_w2a2"], p["rw_a0"],
                      p["rw_kk"], p["rw_ka"], p["rw_rk"], p["rw_ln_w"], p["rw_ln_b"])
    ob = _ssd_branch(pb.reshape(bsz, seq, SSD_COLS_PAD), layer, p["ssd_conv_w"], p["ssd_conv_b"],
                     p["ssd_dt_bias"], p["ssd_a_log"], p["ssd_d"], p["ssd_norm_w"],
                     p["ssd_e64"], p["ssd_e128"])
    oc = _sconv_branch(pc.reshape(bsz, seq, SC_COLS), layer, p["sc_conv_w"])

    wb = _cast_bf16(p["w_branch"].reshape(-1, N_BRANCH * BRANCH_W, D_MODEL), layer)
    outs = _merge_out(oa.reshape(m, BRANCH_W), ob.reshape(m, BRANCH_W), oc.reshape(m, BRANCH_W), pg,
                      wb.reshape(N_BRANCH, BRANCH_W, D_MODEL), _cast_bf16(p["w_out"], layer), x2,
                      next_norm_w, F32 if last else BF16, keep_residual=not last)
    if last:
        return None, outs[0]
    return outs


def kernel(x, norm_w, w_in, rw_mu, rw_w0, rw_w2, rw_a0, rw_a2, rw_kk, rw_ka, rw_rk, rw_ln_w,
           rw_ln_b, ssd_conv_w, ssd_conv_b, ssd_dt_bias, ssd_a_log, ssd_d, ssd_norm_w,
           sc_conv_w, w_branch, w_out, final_norm_w):
    bsz, seq, d = x.shape
    depth = w_in.shape[0]
    zeros = jnp.zeros((depth, RW_RANK, BRANCH_W), F32)
    rw_w2a2 = jnp.concatenate([jnp.concatenate([rw_w2, zeros], axis=2),
                               jnp.concatenate([zeros, rw_a2], axis=2)], axis=1).astype(BF16)
    e64, e128 = _ssd_head_expanders()
    p = dict(w_in=w_in, w_branch=w_branch, w_out=w_out, rw_mu=rw_mu, rw_w0=rw_w0, rw_w2a2=rw_w2a2,
             rw_a0=rw_a0, rw_kk=rw_kk, rw_ka=rw_ka, rw_rk=rw_rk, rw_ln_w=rw_ln_w, rw_ln_b=rw_ln_b,
             ssd_conv_w=ssd_conv_w, ssd_conv_b=ssd_conv_b, ssd_dt_bias=ssd_dt_bias,
             ssd_a_log=ssd_a_log, ssd_d=ssd_d, ssd_norm_w=ssd_norm_w, ssd_e64=e64, ssd_e128=e128,
             sc_conv_w=sc_conv_w)
    x2 = x.reshape(bsz * seq, d)
    h = _rmsnorm(x2, norm_w[0], BF16)
    for l in range(depth):
        last = l == depth - 1
        next_norm_w = final_norm_w if last else norm_w[l + 1]
        x2, h = _layer(x2, h, bsz, seq, next_norm_w, last, l, p)
    return h.reshape(bsz, seq, d)
```

```python
import functools
import math

import jax
import jax.numpy as jnp
from jax import lax
from jax.experimental import pallas as pl
from jax.experimental.pallas import tpu as pltpu

F32 = jnp.float32
BF16 = jnp.bfloat16

D_MODEL = 2048
DEPTH = 2
BRANCH_W = D_MODEL // 2
N_BRANCH = 3
RW_HEAD = 64
RW_HEADS = BRANCH_W // RW_HEAD
RW_RANK = 64
RW_LN_EPS = 64e-5
RW_COLS = 4 * BRANCH_W + 2 * RW_RANK
SSD_HEAD = 64
SSD_HEADS = BRANCH_W // SSD_HEAD
SSD_GROUPS = 4
SSD_STATE = 128
SSD_CONV = 4
SSD_XBC = BRANCH_W + 2 * SSD_GROUPS * SSD_STATE
SSD_COLS = BRANCH_W + SSD_XBC + SSD_HEADS
SSD_NORM_EPS = 1e-5
SC_CONV = 3
SC_COLS = 4 * BRANCH_W
GATE_COLS = N_BRANCH * D_MODEL
NORM_EPS = 1e-6

LANES = 128
SSD_COLS_PAD = 3200
RW_CHUNK = 64
RW_STEP = 256
SSD_CHUNK = 128
SUBLANES = 8
HALO = SUBLANES
VMEM_PHYSICAL = 64 * 1024 * 1024
VMEM_INTERNAL = 8 * 1024 * 1024

assert RW_CHUNK == RW_HEAD and 2 * RW_HEAD == LANES


def _cparams(sem, block_bytes, scratch_bytes=0):
    limit = min(2 * block_bytes + scratch_bytes + VMEM_INTERNAL, VMEM_PHYSICAL - VMEM_INTERNAL)
    return pltpu.CompilerParams(dimension_semantics=sem, vmem_limit_bytes=limit)


def _sigmoid(x):
    return 1.0 / (1.0 + jnp.exp(-x))


def _silu(x):
    return x * _sigmoid(x)


def _softplus(x):
    return jnp.maximum(x, 0.0) + jnp.log1p(jnp.exp(-jnp.abs(x)))


def _dot(a, b):
    return jnp.dot(a, b, preferred_element_type=F32)


def _dot_nt(a, b):
    return lax.dot_general(a, b, (((1,), (1,)), ((), ())), preferred_element_type=F32)


def _dot_tn(a, b):
    return lax.dot_general(a, b, (((0,), (0,)), ((), ())), preferred_element_type=F32)


def _split3(x):
    hi = x.astype(BF16)
    r1 = x - hi.astype(F32)
    mid = r1.astype(BF16)
    lo = (r1 - mid.astype(F32)).astype(BF16)
    return hi, mid, lo


def _dot3(x, m):
    hi, mid, lo = _split3(x)
    return _dot(hi, m) + _dot(mid, m) + _dot(lo, m)


def _dot3_left(m, x):
    hi, mid, lo = _split3(x)
    return _dot(m, hi) + _dot(m, mid) + _dot(m, lo)


def _rmsnorm_kernel(x_ref, w_ref, o_ref):
    x = x_ref[...]
    ms = jnp.mean(x * x, axis=-1, keepdims=True)
    o_ref[...] = (x * lax.rsqrt(ms + NORM_EPS) * w_ref[...]).astype(o_ref.dtype)


def _rmsnorm(x, w, out_dtype, tm=512):
    m, d = x.shape
    blocks = tm * d * (4 + jnp.dtype(out_dtype).itemsize) + d * 4
    return pl.pallas_call(
        _rmsnorm_kernel,
        grid=(m // tm,),
        in_specs=[pl.BlockSpec((tm, d), lambda i: (i, 0)),
                  pl.BlockSpec((1, d), lambda i: (0, 0))],
        out_specs=pl.BlockSpec((tm, d), lambda i: (i, 0)),
        out_shape=jax.ShapeDtypeStruct((m, d), out_dtype),
        compiler_params=_cparams(("parallel",), blocks),
    )(x, w.reshape(1, d))


def _matmul_kernel(a_ref, bt_ref, o_ref):
    o_ref[...] = _dot_nt(a_ref[...], bt_ref[...]).astype(o_ref.dtype)


def _matmul(a, bt, out_dtype, tm, tn):
    m, k = a.shape
    n, _ = bt.shape
    blocks = tm * k * 2 + k * tn * 2 + tm * tn * jnp.dtype(out_dtype).itemsize
    return pl.pallas_call(
        _matmul_kernel,
        grid=(m // tm, n // tn),
        in_specs=[pl.BlockSpec((tm, k), lambda i, j: (i, 0)),
                  pl.BlockSpec((tn, k), lambda i, j: (j, 0))],
        out_specs=pl.BlockSpec((tm, tn), lambda i, j: (i, j)),
        out_shape=jax.ShapeDtypeStruct((m, n), out_dtype),
        compiler_params=_cparams(("parallel", "parallel"), blocks),
    )(a, bt)


def _matmul_shift_kernel(a_ref, b_ref, mu_ref, o_ref, carry_ref, *, tiles_per_seq):
    i = pl.program_id(0)
    j = pl.program_id(1)
    p = _dot_nt(a_ref[...], b_ref[...])
    tm = p.shape[0]
    prev_last = jnp.where(lax.rem(i, tiles_per_seq) == 0, 0.0, carry_ref[j, 0:1, :])
    row = lax.broadcasted_iota(jnp.int32, p.shape, 0)
    shifted = jnp.where(row == 0, prev_last, pltpu.roll(p, 1, axis=0))
    carry_ref[j, 0:1, :] = p[tm - 1:tm, :]
    o_ref[...] = p + mu_ref[...] * (shifted - p)


def _matmul_shift(a, b, mu, layer, seq, tm, tn):
    m, k = a.shape
    n, _ = b.shape
    assert seq % tm == 0
    blocks = tm * k * 2 + k * tn * 2 + tm * tn * 4 + tn * 4
    scratch = (n // tn) * HALO * tn * 4
    return pl.pallas_call(
        functools.partial(_matmul_shift_kernel, tiles_per_seq=seq // tm),
        grid=(m // tm, n // tn),
        in_specs=[pl.BlockSpec((tm, k), lambda i, j: (i, 0)),
                  pl.BlockSpec((tn, k), lambda i, j: (j, 0)),
                  pl.BlockSpec((None, 1, tn), lambda i, j: (layer, 0, j))],
        out_specs=pl.BlockSpec((tm, tn), lambda i, j: (i, j)),
        out_shape=jax.ShapeDtypeStruct((m, n), F32),
        scratch_shapes=[pltpu.VMEM((n // tn, HALO, tn), F32)],
        compiler_params=_cparams(("arbitrary", "arbitrary"), blocks, scratch),
    )(a, b, _rows(mu))


def _cast_rows_kernel(w_ref, o_ref, *, valid_rows):
    tr = o_ref.shape[0]
    row = pl.program_id(0) * tr + lax.broadcasted_iota(jnp.int32, o_ref.shape, 0)
    o_ref[...] = jnp.where(row < valid_rows, w_ref[0], 0.0).astype(o_ref.dtype)


def _cast_rows(wt, layer, start, size, size_pad, tr):
    _, _, k = wt.shape
    assert start % SUBLANES == 0 and tr % SUBLANES == 0 and size_pad % tr == 0
    return pl.pallas_call(
        functools.partial(_cast_rows_kernel, valid_rows=size),
        grid=(size_pad // tr,),
        in_specs=[pl.BlockSpec((pl.Element(1), pl.Element(tr), pl.Element(k)),
                               lambda i: (layer, pl.multiple_of(start + i * tr, SUBLANES), 0))],
        out_specs=pl.BlockSpec((tr, k), lambda i: (i, 0)),
        out_shape=jax.ShapeDtypeStruct((size_pad, k), BF16),
        compiler_params=_cparams(("parallel",), tr * k * 6),
    )(wt)


def _cast_kernel(x_ref, o_ref):
    o_ref[...] = x_ref[...].astype(o_ref.dtype)


def _cast_bf16(x, layer, tr=512):
    _, m, n = x.shape
    return pl.pallas_call(
        _cast_kernel,
        grid=(m // tr,),
        in_specs=[pl.BlockSpec((None, tr, n), lambda i: (layer, i, 0))],
        out_specs=pl.BlockSpec((tr, n), lambda i: (i, 0)),
        out_shape=jax.ShapeDtypeStruct((m, n), BF16),
        compiler_params=_cparams(("parallel",), tr * n * 6),
    )(x)


def _rwkv_kernel(pa_ref, w0_ref, w2a2_ref, a0_ref, kk_ref, ka_ref, rk_ref,
                 lnw_ref, lnb_ref, o_ref, s_ref):
    c = RW_CHUNK
    n = RW_HEAD
    pw = 2 * n
    t = RW_STEP
    nchunks = t // c

    @pl.when(pl.program_id(1) == 0)
    def _():
        s_ref[...] = jnp.zeros_like(s_ref)

    ps = pa_ref[0]

    r = ps[:, 0:BRANCH_W]
    k = ps[:, BRANCH_W:2 * BRANCH_W]
    v = ps[:, 2 * BRANCH_W:3 * BRANCH_W]
    gate = _silu(ps[:, 3 * BRANCH_W:4 * BRANCH_W])
    z = ps[:, 4 * BRANCH_W:4 * BRANCH_W + 2 * RW_RANK]
    zlane = lax.broadcasted_iota(jnp.int32, z.shape, 1)
    zz = jnp.where(zlane < RW_RANK, jnp.tanh(z), z).astype(BF16)
    proj = _dot(zz, w2a2_ref[...])
    ld = -_sigmoid(w0_ref[...] + proj[:, :BRANCH_W]) * math.exp(-0.5)
    a = _sigmoid(a0_ref[...] + proj[:, BRANCH_W:])
    kkv = k * kk_ref[...]
    k2 = k * (1.0 + (a - 1.0) * ka_ref[...])
    rkr = r * k2 * rk_ref[...]

    ti = lax.broadcasted_iota(jnp.int32, (t, t), 0)
    tj = lax.broadcasted_iota(jnp.int32, (t, t), 1)
    same_chunk = (ti // c) == (tj // c)
    tril = jnp.where(jnp.logical_and(same_chunk, ti >= tj), 1.0, 0.0).astype(BF16)
    ld_hi = ld.astype(BF16)
    ld_lo = (ld - ld_hi.astype(F32)).astype(BF16)
    cum = _dot(tril, ld_hi) + _dot(tril, ld_lo)
    e_cum = jnp.exp(cum)
    e_ncum = 1.0 / e_cum
    e_cumx = jnp.exp(cum - ld)

    ci = lax.broadcasted_iota(jnp.int32, (c, pw), 0)
    lane = lax.broadcasted_iota(jnp.int32, (c, pw), 1)
    cj = lane & (c - 1)
    lo = lane < n
    strict = ci > cj
    incl = ci >= cj
    eye = jnp.where(ci == cj, 1.0, 0.0)
    lo_b = jnp.where(lo, 1.0, 0.0).astype(BF16)
    hi_b = jnp.where(lo, 0.0, 1.0).astype(BF16)
    lvl_masks = []
    shift = 0
    while (1 << shift) < c:
        same_big = (ci >> (shift + 1)) == (cj >> (shift + 1))
        diff_small = (ci >> shift) != (cj >> shift)
        lvl_masks.append(jnp.logical_and(same_big, diff_small))
        shift += 1
    bi = lax.broadcasted_iota(jnp.int32, (pw, pw), 0)
    bj = lax.broadcasted_iota(jnp.int32, (pw, pw), 1)
    blockdiag = (bi >= n) == (bj >= n)
    br, bc = bi & (c - 1), bj & (c - 1)
    lvl_bd = []
    for shift in range(1, len(lvl_masks)):
        same_big = (br >> (shift + 1)) == (bc >> (shift + 1))
        diff_small = (br >> shift) != (bc >> shift)
        keep = jnp.logical_and(blockdiag, jnp.logical_and(same_big, diff_small))
        lvl_bd.append(jnp.where(keep, 1.0, 0.0).astype(BF16))

    def bd(y):
        return jnp.concatenate([y * lo_b, y * hi_b], axis=0)

    def half_sums(x):
        s_lo = jnp.sum(jnp.where(lo, x, 0.0), axis=-1, keepdims=True)
        s_hi = jnp.sum(jnp.where(lo, 0.0, x), axis=-1, keepdims=True)
        return jnp.where(lo, s_lo, s_hi)

    pairs = range(RW_HEADS // 2)
    sls = [slice(j * pw, (j + 1) * pw) for j in pairs]
    state = [s_ref[j] for j in pairs]
    outs = []
    for q in range(nchunks):
        rows = slice(q * c, (q + 1) * c)
        blk = lambda arr, sl: arr[rows, sl]
        g_c = jnp.exp(cum[(q + 1) * c - 1:(q + 1) * c, :])
        kh = [blk(kkv, sl) * jnp.minimum(lax.rsqrt(half_sums(blk(kkv, sl) * blk(kkv, sl))), 1e12)
              for sl in sls]
        b2 = [kh[j] * blk(a, sls[j]) for j in pairs]
        e_tot = [blk(e_ncum, sl) * g_c[:, sl] for sl in sls]
        kt = [(kh[j] * blk(e_cumx, sls[j])).astype(BF16) for j in pairs]
        bt = [(b2[j] * blk(e_ncum, sls[j])).astype(BF16) for j in pairs]
        kkt = [(blk(k2, sl) * blk(e_ncum, sl)).astype(BF16) for sl in sls]
        rt = [(blk(r, sl) * blk(e_cum, sl)).astype(BF16) for sl in sls]
        be_n = [(-(b2[j] * e_tot[j])).astype(BF16) for j in pairs]
        ke = [(blk(k2, sls[j]) * e_tot[j]).astype(BF16) for j in pairs]
        vb = [blk(v, sl).astype(BF16) for sl in sls]

        x2 = [jnp.concatenate([kt[j], rt[j]], axis=0) for j in pairs]
        rr = [jnp.concatenate([bt[j] * lo_b, bt[j] * hi_b, kkt[j] * lo_b, kkt[j] * hi_b], axis=0)
              for j in pairs]
        gall = [_dot_nt(x2[j], rr[j]) for j in pairs]
        l_m = [jnp.where(strict, gall[j][:c, :pw], 0.0) for j in pairs]
        akk = [jnp.where(strict, gall[j][:c, pw:], 0.0).astype(BF16) for j in pairs]
        arb_n = [jnp.where(incl, -gall[j][c:, :pw], 0.0).astype(BF16) for j in pairs]
        ark = [jnp.where(incl, gall[j][c:, pw:], 0.0).astype(BF16) for j in pairs]

        minv = [eye - jnp.where(lvl_masks[0], l_m[j], 0.0) for j in pairs]
        l_b = [l_m[j].astype(BF16) for j in pairs]
        l_t = [jnp.concatenate([l_b[j], l_b[j]], axis=0) for j in pairs]
        for m_bd in lvl_bd:
            mb = [minv[j].astype(BF16) for j in pairs]
            t1 = [_dot(mb[j], l_t[j] * m_bd).astype(BF16) for j in pairs]
            minv = [minv[j] - _dot(t1[j], bd(mb[j])) for j in pairs]
        mb = [minv[j].astype(BF16) for j in pairs]

        bdv = [bd(vb[j]) for j in pairs]
        av = [_dot(akk[j], bdv[j]) for j in pairs]
        s_b = [state[j].astype(BF16) for j in pairs]
        ks = [_dot_nt(x2[j], s_b[j]) for j in pairs]
        u_b = [_dot(mb[j], bd((ks[j][:c] + av[j]).astype(BF16))).astype(BF16) for j in pairs]
        y = [ks[j][c:] + _dot(jnp.concatenate([ark[j], arb_n[j]], axis=1),
                              jnp.concatenate([bdv[j], bd(u_b[j])], axis=0)) for j in pairs]
        state = [state[j] * g_c[:, sls[j]]
                 + jnp.where(blockdiag,
                             _dot_tn(jnp.concatenate([vb[j], u_b[j]], axis=0),
                                     jnp.concatenate([ke[j], be_n[j]], axis=0)), 0.0)
                 for j in pairs]

        for j in pairs:
            sl = sls[j]
            yc = y[j] - half_sums(y[j]) * (1.0 / n)
            var = half_sums(yc * yc) * (1.0 / n)
            yn = yc * lax.rsqrt(var + RW_LN_EPS) * lnw_ref[:, sl] + lnb_ref[:, sl]
            bonus = half_sums(blk(rkr, sl)) * blk(v, sl)
            outs.append(((yn + bonus) * blk(gate, sl)).astype(o_ref.dtype))
    for j in pairs:
        s_ref[j] = state[j]
    for q in range(nchunks):
        for j in pairs:
            o_ref[0, q * c:(q + 1) * c, sls[j]] = outs[q * len(sls) + j]


def _rows(p):
    return p.reshape(p.shape[0], 1, -1)


def _layer_spec(shape, layer):
    return pl.BlockSpec((None,) + tuple(shape), lambda *_: (layer,) + (0,) * len(shape))


def _rwkv_branch(pa, layer, w0, w2a2, a0, kk, ka, rk, lnw, lnb):
    bsz, seq, cols = pa.shape
    vec = _layer_spec((1, BRANCH_W), layer)
    blocks = RW_STEP * cols * 4 + 8 * BRANCH_W * 4 + w2a2[0].size * 2 + RW_STEP * BRANCH_W * 2
    scratch = (RW_HEADS // 2) * LANES * LANES * 4
    return pl.pallas_call(
        _rwkv_kernel,
        grid=(bsz, seq // RW_STEP),
        in_specs=[pl.BlockSpec((1, RW_STEP, cols), lambda b, c: (b, c, 0)),
                  vec, _layer_spec((2 * RW_RANK, 2 * BRANCH_W), layer),
                  vec, vec, vec, vec, vec, vec],
        out_specs=pl.BlockSpec((1, RW_STEP, BRANCH_W), lambda b, c: (b, c, 0)),
        out_shape=jax.ShapeDtypeStruct((bsz, seq, BRANCH_W), BF16),
        scratch_shapes=[pltpu.VMEM((RW_HEADS // 2, LANES, LANES), F32)],
        compiler_params=_cparams(("parallel", "arbitrary"), blocks, scratch),
    )(pa, _rows(w0), w2a2, _rows(a0), _rows(kk), _rows(ka), _rows(rk), _rows(lnw), _rows(lnb))


def _ssd_kernel(pb_ref, cw_ref, cb_ref, dtb_ref, alog_ref, d_ref, nw_ref, e64_ref, e128_ref,
                o_ref, ext_ref, st_ref):
    l = SSD_CHUNK
    gw = BRANCH_W // SSD_GROUPS
    hpg = SSD_HEADS // SSD_GROUPS

    @pl.when(pl.program_id(1) == 0)
    def _():
        ext_ref[0:HALO, :] = jnp.zeros((HALO, SSD_XBC), F32)
        st_ref[...] = jnp.zeros_like(st_ref)

    pb = pb_ref[0]
    zgate = _silu(pb[:, 0:BRANCH_W])
    xbc = pb[:, BRANCH_W:BRANCH_W + SSD_XBC]
    dts = pb[:, BRANCH_W + SSD_XBC:BRANCH_W + SSD_XBC + LANES]

    ext_ref[HALO:HALO + l, :] = xbc
    conv = cb_ref[...] + cw_ref[SSD_CONV - 1:SSD_CONV, :] * xbc
    for kk in range(SSD_CONV - 1):
        back = SSD_CONV - 1 - kk
        conv = conv + cw_ref[kk:kk + 1, :] * ext_ref[pl.ds(HALO - back, l), :]
    ext_ref[0:HALO, :] = xbc[l - HALO:l, :]
    xa = _silu(conv)
    xs = xa[:, 0:BRANCH_W]
    bm = xa[:, BRANCH_W:BRANCH_W + SSD_GROUPS * SSD_STATE]
    cm = xa[:, BRANCH_W + SSD_GROUPS * SSD_STATE:]

    dt16 = _softplus(dts + dtb_ref[...])
    da16 = dt16 * (-jnp.exp(alog_ref[...]))
    ri = lax.broadcasted_iota(jnp.int32, (l, l), 0)
    rj = lax.broadcasted_iota(jnp.int32, (l, l), 1)
    causal = ri >= rj
    tril = jnp.where(causal, 1.0, 0.0).astype(BF16)
    acs16 = _dot3_left(tril, da16)
    acs_t = acs16.T
    e64 = e64_ref[...]
    dt_full = _dot3(dt16, e64)
    acs_full = _dot3(acs16, e64)
    acs_b = _dot3(acs16, e128_ref[...])
    tot_full = acs_full[l - 1:l, :]
    dte = jnp.exp(tot_full - acs_full)
    eacs = jnp.exp(acs_full)
    dec = jnp.exp(tot_full)
    xdt = xs * dt_full
    xdte = (xdt * dte).astype(BF16)
    glane = lax.broadcasted_iota(jnp.int32, (l, gw), 1) // SSD_HEAD

    outs = []
    st_new = []
    for g in range(SSD_GROUPS):
        gs = slice(g * gw, (g + 1) * gw)
        c_g = cm[:, g * SSD_STATE:(g + 1) * SSD_STATE].astype(BF16)
        b_g = bm[:, g * SSD_STATE:(g + 1) * SSD_STATE].astype(BF16)
        scores = _dot_nt(c_g, b_g)
        xdt_g = xdt[:, gs]
        p_parts = []
        x_parts = []
        for jj in range(hpg):
            j = g * hpg + jj
            diff = acs_b[:, j * LANES:(j + 1) * LANES] - acs_t[j:j + 1, :]
            seg = jnp.exp(jnp.where(causal, diff, -jnp.inf))
            p_parts.append((scores * seg).astype(BF16))
            x_parts.append(jnp.where(glane == jj, xdt_g, 0.0).astype(BF16))
        y_diag = _dot(jnp.concatenate(p_parts, axis=1), jnp.concatenate(x_parts, axis=0))
        st = st_ref[g]
        y_off = _dot(c_g, st.astype(BF16)) * eacs[:, gs]
        st_new.append(st * dec[:, gs] + _dot_tn(b_g, xdte[:, gs]))
        y = (y_diag + y_off + d_ref[:, gs] * xs[:, gs]) * zgate[:, gs]
        ms = jnp.mean(y * y, axis=-1, keepdims=True)
        outs.append((y * lax.rsqrt(ms + SSD_NORM_EPS) * nw_ref[:, gs]).astype(o_ref.dtype))
    for g in range(SSD_GROUPS):
        st_ref[g] = st_new[g]
    for g in range(SSD_GROUPS):
        o_ref[0, :, g * gw:(g + 1) * gw] = outs[g]


def _ssd_head_expanders():
    head_of_lane64 = jnp.arange(BRANCH_W) // SSD_HEAD
    head_of_lane128 = jnp.arange(SSD_HEADS * LANES) // LANES
    e64 = (jnp.arange(LANES)[:, None] == head_of_lane64[None, :]).astype(BF16)
    e128 = (jnp.arange(LANES)[:, None] == head_of_lane128[None, :]).astype(BF16)
    return e64, e128


def _ssd_branch(pb, layer, conv_w, conv_b, dt_bias, a_log, d_skip, norm_w, e64, e128):
    bsz, seq, cols = pb.shape
    pad16 = lambda t: jnp.pad(t, ((0, 0), (0, LANES - SSD_HEADS)))
    d_full = jnp.repeat(d_skip, SSD_HEAD, axis=1)
    const = lambda shape: pl.BlockSpec(shape, lambda b, c: (0, 0))
    blocks = (SSD_CHUNK * cols * 4 + (SSD_CONV + 1) * SSD_XBC * 4 + 4 * BRANCH_W * 4
              + e64.size * 2 + e128.size * 2 + SSD_CHUNK * BRANCH_W * 2)
    scratch = (HALO + SSD_CHUNK) * SSD_XBC * 4 + SSD_STATE * BRANCH_W * 4
    return pl.pallas_call(
        _ssd_kernel,
        grid=(bsz, seq // SSD_CHUNK),
        in_specs=[pl.BlockSpec((1, SSD_CHUNK, cols), lambda b, c: (b, c, 0)),
                  _layer_spec((SSD_CONV, SSD_XBC), layer), _layer_spec((1, SSD_XBC), layer),
                  _layer_spec((1, LANES), layer), _layer_spec((1, LANES), layer),
                  _layer_spec((1, BRANCH_W), layer), _layer_spec((1, BRANCH_W), layer),
                  const((LANES, BRANCH_W)), const((LANES, SSD_HEADS * LANES))],
        out_specs=pl.BlockSpec((1, SSD_CHUNK, BRANCH_W), lambda b, c: (b, c, 0)),
        out_shape=jax.ShapeDtypeStruct((bsz, seq, BRANCH_W), BF16),
        scratch_shapes=[pltpu.VMEM((HALO + SSD_CHUNK, SSD_XBC), F32),
                        pltpu.VMEM((SSD_GROUPS, SSD_STATE, BRANCH_W // SSD_GROUPS), F32)],
        compiler_params=_cparams(("parallel", "arbitrary"), blocks, scratch),
    )(pb, conv_w, _rows(conv_b), _rows(pad16(dt_bias)), _rows(pad16(a_log)), _rows(d_full),
      _rows(norm_w), e64, e128)


def _sconv_kernel(pc_ref, cw_ref, o_ref, ext_ref):
    t = pc_ref.shape[1]

    @pl.when(pl.program_id(1) == 0)
    def _():
        ext_ref[0:HALO, :] = jnp.zeros((HALO, BRANCH_W), F32)

    pc = pc_ref[0].astype(F32)
    bg = pc[:, 0:BRANCH_W]
    u = pc[:, BRANCH_W:2 * BRANCH_W] * pc[:, 2 * BRANCH_W:3 * BRANCH_W]
    gate = _silu(pc[:, 3 * BRANCH_W:4 * BRANCH_W])
    ext_ref[HALO:HALO + t, :] = u
    conv = cw_ref[SC_CONV - 1:SC_CONV, :] * u
    for kk in range(SC_CONV - 1):
        back = SC_CONV - 1 - kk
        conv = conv + cw_ref[kk:kk + 1, :] * ext_ref[pl.ds(HALO - back, t), :]
    ext_ref[0:HALO, :] = u[t - HALO:t, :]
    o_ref[0] = (bg * conv * gate).astype(o_ref.dtype)


def _sconv_branch(pc, layer, conv_w, tm=512):
    bsz, seq, cols = pc.shape
    blocks = tm * cols * pc.dtype.itemsize + SC_CONV * BRANCH_W * 4 + tm * BRANCH_W * 2
    scratch = (HALO + tm) * BRANCH_W * 4
    return pl.pallas_call(
        _sconv_kernel,
        grid=(bsz, seq // tm),
        in_specs=[pl.BlockSpec((1, tm, cols), lambda b, c: (b, c, 0)),
                  _layer_spec((SC_CONV, BRANCH_W), layer)],
        out_specs=pl.BlockSpec((1, tm, BRANCH_W), lambda b, c: (b, c, 0)),
        out_shape=jax.ShapeDtypeStruct((bsz, seq, BRANCH_W), BF16),
        scratch_shapes=[pltpu.VMEM((HALO + tm, BRANCH_W), F32)],
        compiler_params=_cparams(("parallel", "arbitrary"), blocks, scratch),
    )(pc, conv_w)


def _merge_out_kernel(oa_ref, ob_ref, oc_ref, ga_ref, gb_ref, gc_ref, wb_ref, wo_ref, x_ref,
                      nw_ref, *o_refs):
    acc = _sigmoid(ga_ref[...].astype(F32)) * _dot(oa_ref[...], wb_ref[0])
    acc = acc + _sigmoid(gb_ref[...].astype(F32)) * _dot(ob_ref[...], wb_ref[1])
    acc = acc + _sigmoid(gc_ref[...].astype(F32)) * _dot(oc_ref[...], wb_ref[2])
    xn = x_ref[...] + _dot(acc.astype(BF16), wo_ref[...])
    ms = jnp.mean(xn * xn, axis=-1, keepdims=True)
    h_ref = o_refs[-1]
    h_ref[...] = (xn * lax.rsqrt(ms + NORM_EPS) * nw_ref[...]).astype(h_ref.dtype)
    if len(o_refs) == 2:
        o_refs[0][...] = xn


def _merge_out(oa, ob, oc, pg, wb, wo, x, nw, norm_dtype, keep_residual, tm=256):
    m = oa.shape[0]
    d = D_MODEL
    row = lambda i: (i, 0)
    once = pl.Buffered(1)
    o_spec = pl.BlockSpec((tm, BRANCH_W), row)
    g_spec = lambda nb: pl.BlockSpec((tm, d), lambda i: (i, nb))
    norm_bytes = jnp.dtype(norm_dtype).itemsize
    blocks = (N_BRANCH * tm * BRANCH_W * 2 + N_BRANCH * tm * d * pg.dtype.itemsize
              + tm * d * (4 + norm_bytes) + d * 4)
    out_specs = [pl.BlockSpec((tm, d), row)]
    out_shape = [jax.ShapeDtypeStruct((m, d), norm_dtype)]
    if keep_residual:
        blocks += tm * d * 4
        out_specs.insert(0, pl.BlockSpec((tm, d), row))
        out_shape.insert(0, jax.ShapeDtypeStruct((m, d), F32))
    resident = wb.size * 2 + wo.size * 2 + tm * d * 6
    return pl.pallas_call(
        _merge_out_kernel,
        grid=(m // tm,),
        in_specs=[o_spec, o_spec, o_spec, g_spec(0), g_spec(1), g_spec(2),
                  pl.BlockSpec((N_BRANCH, BRANCH_W, d), lambda i: (0, 0, 0), pipeline_mode=once),
                  pl.BlockSpec((d, d), lambda i: (0, 0), pipeline_mode=once),
                  pl.BlockSpec((tm, d), row),
                  pl.BlockSpec((1, d), lambda i: (0, 0))],
        out_specs=out_specs,
        out_shape=out_shape,
        compiler_params=_cparams(("parallel",), blocks, resident),
    )(oa, ob, oc, pg, pg, pg, wb, wo, x, nw.reshape(1, d))


def _layer(x2, h, bsz, seq, next_norm_w, last, layer, p):
    m = bsz * seq
    wt = p["w_in_t"]
    c0, c1, c2 = RW_COLS, RW_COLS + SSD_COLS, RW_COLS + SSD_COLS + SC_COLS
    w_a = _cast_rows(wt, layer, 0, RW_COLS, RW_COLS, tr=1056)
    w_b = _cast_rows(wt, layer, c0, SSD_COLS, SSD_COLS_PAD, tr=640)
    w_c = _cast_rows(wt, layer, c1, SC_COLS, SC_COLS, tr=1024)
    w_g = _cast_rows(wt, layer, c2, GATE_COLS, GATE_COLS, tr=1024)

    pa = _matmul_shift(h, w_a, p["rw_mu"], layer, seq, tm=1024, tn=1408)
    pb = _matmul(h, w_b, F32, tm=512, tn=SSD_COLS_PAD)
    pc = _matmul(h, w_c, BF16, tm=1024, tn=2048)
    pg = _matmul(h, w_g, BF16, tm=1024, tn=2048)

    oa = _rwkv_branch(pa.reshape(bsz, seq, RW_COLS), layer, p["rw_w0"], p["rw_w2a2"], p["rw_a0"],
                      p["rw_kk"], p["rw_ka"], p["rw_rk"], p["rw_ln_w"], p["rw_ln_b"])
    ob = _ssd_branch(pb.reshape(bsz, seq, SSD_COLS_PAD), layer, p["ssd_conv_w"], p["ssd_conv_b"],
                     p["ssd_dt_bias"], p["ssd_a_log"], p["ssd_d"], p["ssd_norm_w"],
                     p["ssd_e64"], p["ssd_e128"])
    oc = _sconv_branch(pc.reshape(bsz, seq, SC_COLS), layer, p["sc_conv_w"])

    wb = _cast_bf16(p["w_branch"].reshape(-1, N_BRANCH * BRANCH_W, D_MODEL), layer)
    outs = _merge_out(oa.reshape(m, BRANCH_W), ob.reshape(m, BRANCH_W), oc.reshape(m, BRANCH_W), pg,
                      wb.reshape(N_BRANCH, BRANCH_W, D_MODEL), _cast_bf16(p["w_out"], layer), x2,
                      next_norm_w, F32 if last else BF16, keep_residual=not last)
    if last:
        return None, outs[0]
    return outs


def kernel(x, norm_w, w_in, rw_mu, rw_w0, rw_w2, rw_a0, rw_a2, rw_kk, rw_ka, rw_rk, rw_ln_w,
           rw_ln_b, ssd_conv_w, ssd_conv_b, ssd_dt_bias, ssd_a_log, ssd_d, ssd_norm_w,
           sc_conv_w, w_branch, w_out, final_norm_w):
    bsz, seq, d = x.shape
    depth = w_in.shape[0]
    zeros = jnp.zeros((depth, RW_RANK, BRANCH_W), F32)
    rw_w2a2 = jnp.concatenate([jnp.concatenate([rw_w2, zeros], axis=2),
                               jnp.concatenate([zeros, rw_a2], axis=2)], axis=1).astype(BF16)
    e64, e128 = _ssd_head_expanders()
    p = dict(w_in_t=jnp.transpose(w_in, (0, 2, 1)), w_branch=w_branch, w_out=w_out, rw_mu=rw_mu,
             rw_w0=rw_w0, rw_w2a2=rw_w2a2,
             rw_a0=rw_a0, rw_kk=rw_kk, rw_ka=rw_ka, rw_rk=rw_rk, rw_ln_w=rw_ln_w, rw_ln_b=rw_ln_b,
             ssd_conv_w=ssd_conv_w, ssd_conv_b=ssd_conv_b, ssd_dt_bias=ssd_dt_bias,
             ssd_a_log=ssd_a_log, ssd_d=ssd_d, ssd_norm_w=ssd_norm_w, ssd_e64=e64, ssd_e128=e128,
             sc_conv_w=sc_conv_w)
    x2 = x.reshape(bsz * seq, d)
    h = _rmsnorm(x2, norm_w[0], BF16)
    for l in range(depth):
        last = l == depth - 1
        next_norm_w = final_norm_w if last else norm_w[l + 1]
        x2, h = _layer(x2, h, bsz, seq, next_norm_w, last, l, p)
    return h.reshape(bsz, seq, d)
```

```python
import functools
import math

import jax
import jax.numpy as jnp
from jax import lax
from jax.experimental import pallas as pl
from jax.experimental.pallas import tpu as pltpu

F32 = jnp.float32
BF16 = jnp.bfloat16

D_MODEL = 2048
DEPTH = 2
BRANCH_W = D_MODEL // 2
N_BRANCH = 3
RW_HEAD = 64
RW_HEADS = BRANCH_W // RW_HEAD
RW_RANK = 64
RW_LN_EPS = 64e-5
RW_COLS = 4 * BRANCH_W + 2 * RW_RANK
SSD_HEAD = 64
SSD_HEADS = BRANCH_W // SSD_HEAD
SSD_GROUPS = 4
SSD_STATE = 128
SSD_CONV = 4
SSD_XBC = BRANCH_W + 2 * SSD_GROUPS * SSD_STATE
SSD_COLS = BRANCH_W + SSD_XBC + SSD_HEADS
SSD_NORM_EPS = 1e-5
SC_CONV = 3
SC_COLS = 4 * BRANCH_W
GATE_COLS = N_BRANCH * D_MODEL
NORM_EPS = 1e-6
LOG2E = math.log2(math.e)

LANES = 128
SSD_COLS_PAD = 3200
RW_CHUNK = 64
RW_STEP = 256
SSD_CHUNK = 128
SUBLANES = 8
HALO = SUBLANES
VMEM_PHYSICAL = 64 * 1024 * 1024
VMEM_INTERNAL = 8 * 1024 * 1024

assert RW_CHUNK == RW_HEAD and 2 * RW_HEAD == LANES


def _cparams(sem, block_bytes, scratch_bytes=0):
    limit = min(2 * block_bytes + scratch_bytes + VMEM_INTERNAL, VMEM_PHYSICAL - VMEM_INTERNAL)
    return pltpu.CompilerParams(dimension_semantics=sem, vmem_limit_bytes=limit)


def _sigmoid(x):
    return 1.0 / (1.0 + jnp.exp(-x))


def _silu(x):
    return x * _sigmoid(x)


def _softplus(x):
    return jnp.maximum(x, 0.0) + jnp.log1p(jnp.exp(-jnp.abs(x)))


def _dot(a, b):
    return jnp.dot(a, b, preferred_element_type=F32)


def _dot_nt(a, b):
    return lax.dot_general(a, b, (((1,), (1,)), ((), ())), preferred_element_type=F32)


def _dot_tn(a, b):
    return lax.dot_general(a, b, (((0,), (0,)), ((), ())), preferred_element_type=F32)


def _split3(x):
    hi = x.astype(BF16)
    r1 = x - hi.astype(F32)
    mid = r1.astype(BF16)
    lo = (r1 - mid.astype(F32)).astype(BF16)
    return hi, mid, lo


def _dot3(x, m):
    hi, mid, lo = _split3(x)
    return _dot(hi, m) + _dot(mid, m) + _dot(lo, m)


def _dot3_left(m, x):
    hi, mid, lo = _split3(x)
    return _dot(m, hi) + _dot(m, mid) + _dot(m, lo)


def _rmsnorm_kernel(x_ref, w_ref, o_ref):
    x = x_ref[...]
    ms = jnp.mean(x * x, axis=-1, keepdims=True)
    o_ref[...] = (x * lax.rsqrt(ms + NORM_EPS) * w_ref[...]).astype(o_ref.dtype)


def _rmsnorm(x, w, out_dtype, tm=512):
    m, d = x.shape
    blocks = tm * d * (4 + jnp.dtype(out_dtype).itemsize) + d * 4
    return pl.pallas_call(
        _rmsnorm_kernel,
        grid=(m // tm,),
        in_specs=[pl.BlockSpec((tm, d), lambda i: (i, 0)),
                  pl.BlockSpec((1, d), lambda i: (0, 0))],
        out_specs=pl.BlockSpec((tm, d), lambda i: (i, 0)),
        out_shape=jax.ShapeDtypeStruct((m, d), out_dtype),
        compiler_params=_cparams(("parallel",), blocks),
    )(x, w.reshape(1, d))


def _matmul_kernel(a_ref, bt_ref, o_ref):
    o_ref[...] = _dot_nt(a_ref[...], bt_ref[...]).astype(o_ref.dtype)


def _matmul(a, bt, out_dtype, tm, tn):
    m, k = a.shape
    n, _ = bt.shape
    blocks = tm * k * 2 + k * tn * 2 + tm * tn * jnp.dtype(out_dtype).itemsize
    return pl.pallas_call(
        _matmul_kernel,
        grid=(m // tm, n // tn),
        in_specs=[pl.BlockSpec((tm, k), lambda i, j: (i, 0)),
                  pl.BlockSpec((tn, k), lambda i, j: (j, 0))],
        out_specs=pl.BlockSpec((tm, tn), lambda i, j: (i, j)),
        out_shape=jax.ShapeDtypeStruct((m, n), out_dtype),
        compiler_params=_cparams(("parallel", "parallel"), blocks),
    )(a, bt)


def _matmul_shift_kernel(a_ref, b_ref, mu_ref, o_ref, carry_ref, *, tiles_per_seq):
    i = pl.program_id(0)
    j = pl.program_id(1)
    p = _dot_nt(a_ref[...], b_ref[...])
    tm = p.shape[0]
    prev_last = jnp.where(lax.rem(i, tiles_per_seq) == 0, 0.0, carry_ref[j, 0:1, :])
    row = lax.broadcasted_iota(jnp.int32, p.shape, 0)
    shifted = jnp.where(row == 0, prev_last, pltpu.roll(p, 1, axis=0))
    carry_ref[j, 0:1, :] = p[tm - 1:tm, :]
    o_ref[...] = p + mu_ref[...] * (shifted - p)


def _matmul_shift(a, b, mu, layer, seq, tm, tn):
    m, k = a.shape
    n, _ = b.shape
    assert seq % tm == 0
    blocks = tm * k * 2 + k * tn * 2 + tm * tn * 4 + tn * 4
    scratch = (n // tn) * HALO * tn * 4
    return pl.pallas_call(
        functools.partial(_matmul_shift_kernel, tiles_per_seq=seq // tm),
        grid=(m // tm, n // tn),
        in_specs=[pl.BlockSpec((tm, k), lambda i, j: (i, 0)),
                  pl.BlockSpec((tn, k), lambda i, j: (j, 0)),
                  pl.BlockSpec((None, 1, tn), lambda i, j: (layer, 0, j))],
        out_specs=pl.BlockSpec((tm, tn), lambda i, j: (i, j)),
        out_shape=jax.ShapeDtypeStruct((m, n), F32),
        scratch_shapes=[pltpu.VMEM((n // tn, HALO, tn), F32)],
        compiler_params=_cparams(("arbitrary", "arbitrary"), blocks, scratch),
    )(a, b, _rows(mu))


def _cast_rows_kernel(w_ref, o_ref, *, valid_rows):
    tr = o_ref.shape[0]
    row = pl.program_id(0) * tr + lax.broadcasted_iota(jnp.int32, o_ref.shape, 0)
    o_ref[...] = jnp.where(row < valid_rows, w_ref[0], 0.0).astype(o_ref.dtype)


def _cast_rows(wt, layer, start, size, size_pad, tr):
    _, _, k = wt.shape
    assert start % SUBLANES == 0 and tr % SUBLANES == 0 and size_pad % tr == 0
    return pl.pallas_call(
        functools.partial(_cast_rows_kernel, valid_rows=size),
        grid=(size_pad // tr,),
        in_specs=[pl.BlockSpec((pl.Element(1), pl.Element(tr), pl.Element(k)),
                               lambda i: (layer, pl.multiple_of(start + i * tr, SUBLANES), 0))],
        out_specs=pl.BlockSpec((tr, k), lambda i: (i, 0)),
        out_shape=jax.ShapeDtypeStruct((size_pad, k), BF16),
        compiler_params=_cparams(("parallel",), tr * k * 6),
    )(wt)


def _cast_kernel(x_ref, o_ref):
    o_ref[...] = x_ref[...].astype(o_ref.dtype)


def _cast_bf16(x, layer, tr=512):
    _, m, n = x.shape
    return pl.pallas_call(
        _cast_kernel,
        grid=(m // tr,),
        in_specs=[pl.BlockSpec((None, tr, n), lambda i: (layer, i, 0))],
        out_specs=pl.BlockSpec((tr, n), lambda i: (i, 0)),
        out_shape=jax.ShapeDtypeStruct((m, n), BF16),
        compiler_params=_cparams(("parallel",), tr * n * 6),
    )(x)


def _rwkv_kernel(pa_ref, w0_ref, w2a2_ref, a0_ref, kk_ref, ka_ref, rk_ref,
                 lnw_ref, lnb_ref, o_ref, s_ref):
    c = RW_CHUNK
    n = RW_HEAD
    pw = 2 * n
    t = RW_STEP
    nchunks = t // c

    @pl.when(pl.program_id(1) == 0)
    def _():
        s_ref[...] = jnp.zeros_like(s_ref)

    ps = pa_ref[0]

    r = ps[:, 0:BRANCH_W]
    k = ps[:, BRANCH_W:2 * BRANCH_W]
    v = ps[:, 2 * BRANCH_W:3 * BRANCH_W]
    gate = _silu(ps[:, 3 * BRANCH_W:4 * BRANCH_W])
    z = ps[:, 4 * BRANCH_W:4 * BRANCH_W + 2 * RW_RANK]
    zlane = lax.broadcasted_iota(jnp.int32, z.shape, 1)
    zz = jnp.where(zlane < RW_RANK, jnp.tanh(z), z).astype(BF16)
    proj = _dot(zz, w2a2_ref[...])
    ld = -_sigmoid(w0_ref[...] + proj[:, :BRANCH_W]) * math.exp(-0.5)
    a = _sigmoid(a0_ref[...] + proj[:, BRANCH_W:])
    kkv = k * kk_ref[...]
    k2 = k * (1.0 + (a - 1.0) * ka_ref[...])
    rkr = r * k2 * rk_ref[...]

    ti = lax.broadcasted_iota(jnp.int32, (t, t), 0)
    tj = lax.broadcasted_iota(jnp.int32, (t, t), 1)
    same_chunk = (ti // c) == (tj // c)
    tril = jnp.where(jnp.logical_and(same_chunk, ti >= tj), 1.0, 0.0).astype(BF16)
    ld_hi = ld.astype(BF16)
    ld_lo = (ld - ld_hi.astype(F32)).astype(BF16)
    cum = _dot(tril, ld_hi) + _dot(tril, ld_lo)
    e_cum = jnp.exp(cum)
    e_ncum = 1.0 / e_cum
    e_cumx = jnp.exp(cum - ld)

    ci = lax.broadcasted_iota(jnp.int32, (c, pw), 0)
    lane = lax.broadcasted_iota(jnp.int32, (c, pw), 1)
    cj = lane & (c - 1)
    lo = lane < n
    strict = ci > cj
    incl = ci >= cj
    eye = jnp.where(ci == cj, 1.0, 0.0)
    lo_b = jnp.where(lo, 1.0, 0.0).astype(BF16)
    hi_b = jnp.where(lo, 0.0, 1.0).astype(BF16)
    lvl_masks = []
    shift = 0
    while (1 << shift) < c:
        same_big = (ci >> (shift + 1)) == (cj >> (shift + 1))
        diff_small = (ci >> shift) != (cj >> shift)
        lvl_masks.append(jnp.logical_and(same_big, diff_small))
        shift += 1
    bi = lax.broadcasted_iota(jnp.int32, (pw, pw), 0)
    bj = lax.broadcasted_iota(jnp.int32, (pw, pw), 1)
    blockdiag = (bi >= n) == (bj >= n)
    br, bc = bi & (c - 1), bj & (c - 1)
    lvl_bd = []
    for shift in range(1, len(lvl_masks)):
        same_big = (br >> (shift + 1)) == (bc >> (shift + 1))
        diff_small = (br >> shift) != (bc >> shift)
        keep = jnp.logical_and(blockdiag, jnp.logical_and(same_big, diff_small))
        lvl_bd.append(jnp.where(keep, 1.0, 0.0).astype(BF16))

    def bd(y):
        return jnp.concatenate([y * lo_b, y * hi_b], axis=0)

    def half_sums(x):
        s_lo = jnp.sum(jnp.where(lo, x, 0.0), axis=-1, keepdims=True)
        s_hi = jnp.sum(jnp.where(lo, 0.0, x), axis=-1, keepdims=True)
        return jnp.where(lo, s_lo, s_hi)

    pairs = range(RW_HEADS // 2)
    sls = [slice(j * pw, (j + 1) * pw) for j in pairs]
    state = [s_ref[j] for j in pairs]
    outs = []
    for q in range(nchunks):
        rows = slice(q * c, (q + 1) * c)
        blk = lambda arr, sl: arr[rows, sl]
        g_c = jnp.exp(cum[(q + 1) * c - 1:(q + 1) * c, :])
        kh = [blk(kkv, sl) * jnp.minimum(lax.rsqrt(half_sums(blk(kkv, sl) * blk(kkv, sl))), 1e12)
              for sl in sls]
        b2 = [kh[j] * blk(a, sls[j]) for j in pairs]
        e_tot = [blk(e_ncum, sl) * g_c[:, sl] for sl in sls]
        kt = [(kh[j] * blk(e_cumx, sls[j])).astype(BF16) for j in pairs]
        bt = [(b2[j] * blk(e_ncum, sls[j])).astype(BF16) for j in pairs]
        kkt = [(blk(k2, sl) * blk(e_ncum, sl)).astype(BF16) for sl in sls]
        rt = [(blk(r, sl) * blk(e_cum, sl)).astype(BF16) for sl in sls]
        be_n = [(-(b2[j] * e_tot[j])).astype(BF16) for j in pairs]
        ke = [(blk(k2, sls[j]) * e_tot[j]).astype(BF16) for j in pairs]
        vb = [blk(v, sl).astype(BF16) for sl in sls]

        x2 = [jnp.concatenate([kt[j], rt[j]], axis=0) for j in pairs]
        rr = [jnp.concatenate([bt[j] * lo_b, bt[j] * hi_b, kkt[j] * lo_b, kkt[j] * hi_b], axis=0)
              for j in pairs]
        gall = [_dot_nt(x2[j], rr[j]) for j in pairs]
        l_m = [jnp.where(strict, gall[j][:c, :pw], 0.0) for j in pairs]
        akk = [jnp.where(strict, gall[j][:c, pw:], 0.0).astype(BF16) for j in pairs]
        arb_n = [jnp.where(incl, -gall[j][c:, :pw], 0.0).astype(BF16) for j in pairs]
        ark = [jnp.where(incl, gall[j][c:, pw:], 0.0).astype(BF16) for j in pairs]

        minv = [eye - jnp.where(lvl_masks[0], l_m[j], 0.0) for j in pairs]
        l_b = [l_m[j].astype(BF16) for j in pairs]
        l_t = [jnp.concatenate([l_b[j], l_b[j]], axis=0) for j in pairs]
        for m_bd in lvl_bd:
            mb = [minv[j].astype(BF16) for j in pairs]
            t1 = [_dot(mb[j], l_t[j] * m_bd).astype(BF16) for j in pairs]
            minv = [minv[j] - _dot(t1[j], bd(mb[j])) for j in pairs]
        mb = [minv[j].astype(BF16) for j in pairs]

        bdv = [bd(vb[j]) for j in pairs]
        av = [_dot(akk[j], bdv[j]) for j in pairs]
        s_b = [state[j].astype(BF16) for j in pairs]
        ks = [_dot_nt(x2[j], s_b[j]) for j in pairs]
        u_b = [_dot(mb[j], bd((ks[j][:c] + av[j]).astype(BF16))).astype(BF16) for j in pairs]
        y = [ks[j][c:] + _dot(jnp.concatenate([ark[j], arb_n[j]], axis=1),
                              jnp.concatenate([bdv[j], bd(u_b[j])], axis=0)) for j in pairs]
        state = [state[j] * g_c[:, sls[j]]
                 + jnp.where(blockdiag,
                             _dot_tn(jnp.concatenate([vb[j], u_b[j]], axis=0),
                                     jnp.concatenate([ke[j], be_n[j]], axis=0)), 0.0)
                 for j in pairs]

        for j in pairs:
            sl = sls[j]
            yc = y[j] - half_sums(y[j]) * (1.0 / n)
            var = half_sums(yc * yc) * (1.0 / n)
            yn = yc * lax.rsqrt(var + RW_LN_EPS) * lnw_ref[:, sl] + lnb_ref[:, sl]
            bonus = half_sums(blk(rkr, sl)) * blk(v, sl)
            outs.append(((yn + bonus) * blk(gate, sl)).astype(o_ref.dtype))
    for j in pairs:
        s_ref[j] = state[j]
    for q in range(nchunks):
        for j in pairs:
            o_ref[0, q * c:(q + 1) * c, sls[j]] = outs[q * len(sls) + j]


def _rows(p):
    return p.reshape(p.shape[0], 1, -1)


def _layer_spec(shape, layer):
    return pl.BlockSpec((None,) + tuple(shape), lambda *_: (layer,) + (0,) * len(shape))


def _rwkv_branch(pa, layer, w0, w2a2, a0, kk, ka, rk, lnw, lnb):
    bsz, seq, cols = pa.shape
    vec = _layer_spec((1, BRANCH_W), layer)
    blocks = RW_STEP * cols * 4 + 8 * BRANCH_W * 4 + w2a2[0].size * 2 + RW_STEP * BRANCH_W * 2
    scratch = (RW_HEADS // 2) * LANES * LANES * 4
    return pl.pallas_call(
        _rwkv_kernel,
        grid=(bsz, seq // RW_STEP),
        in_specs=[pl.BlockSpec((1, RW_STEP, cols), lambda b, c: (b, c, 0)),
                  vec, _layer_spec((2 * RW_RANK, 2 * BRANCH_W), layer),
                  vec, vec, vec, vec, vec, vec],
        out_specs=pl.BlockSpec((1, RW_STEP, BRANCH_W), lambda b, c: (b, c, 0)),
        out_shape=jax.ShapeDtypeStruct((bsz, seq, BRANCH_W), BF16),
        scratch_shapes=[pltpu.VMEM((RW_HEADS // 2, LANES, LANES), F32)],
        compiler_params=_cparams(("parallel", "arbitrary"), blocks, scratch),
    )(pa, _rows(w0), w2a2, _rows(a0), _rows(kk), _rows(ka), _rows(rk), _rows(lnw), _rows(lnb))


def _ssd_kernel(pb_ref, cw_ref, cb_ref, dtb_ref, alog_ref, d_ref, nw_ref, e64_ref, e128_ref,
                o_ref, ext_ref, st_ref):
    l = SSD_CHUNK
    gw = BRANCH_W // SSD_GROUPS
    hpg = SSD_HEADS // SSD_GROUPS

    @pl.when(pl.program_id(1) == 0)
    def _():
        ext_ref[0:HALO, :] = jnp.zeros((HALO, SSD_XBC), F32)
        st_ref[...] = jnp.zeros_like(st_ref)

    pb = pb_ref[0]
    zgate = _silu(pb[:, 0:BRANCH_W])
    xbc = pb[:, BRANCH_W:BRANCH_W + SSD_XBC]
    dts = pb[:, BRANCH_W + SSD_XBC:BRANCH_W + SSD_XBC + LANES]

    ext_ref[HALO:HALO + l, :] = xbc
    conv = cb_ref[...] + cw_ref[SSD_CONV - 1:SSD_CONV, :] * xbc
    for kk in range(SSD_CONV - 1):
        back = SSD_CONV - 1 - kk
        conv = conv + cw_ref[kk:kk + 1, :] * ext_ref[pl.ds(HALO - back, l), :]
    ext_ref[0:HALO, :] = xbc[l - HALO:l, :]
    xa = _silu(conv)
    xs = xa[:, 0:BRANCH_W]
    bm = xa[:, BRANCH_W:BRANCH_W + SSD_GROUPS * SSD_STATE]
    cm = xa[:, BRANCH_W + SSD_GROUPS * SSD_STATE:]

    dt16 = _softplus(dts + dtb_ref[...])
    da16 = dt16 * (-jnp.exp(alog_ref[...]))
    ri = lax.broadcasted_iota(jnp.int32, (l, l), 0)
    rj = lax.broadcasted_iota(jnp.int32, (l, l), 1)
    causal = ri >= rj
    tril = jnp.where(causal, 1.0, 0.0).astype(BF16)
    acs16 = _dot3_left(tril, da16 * LOG2E)
    acs_t = acs16.T
    e64 = e64_ref[...]
    dt_full = _dot3(dt16, e64)
    acs_full = _dot3(acs16, e64)
    acs_b = _dot3(acs16, e128_ref[...])
    tot_full = acs_full[l - 1:l, :]
    dte = jnp.exp2(tot_full - acs_full)
    eacs = jnp.exp2(acs_full)
    dec = jnp.exp2(tot_full)
    xdt = xs * dt_full
    xdte = (xdt * dte).astype(BF16)
    glane = lax.broadcasted_iota(jnp.int32, (l, gw), 1) // SSD_HEAD
    head_b = [jnp.where(glane == jj, 1.0, 0.0).astype(BF16) for jj in range(hpg)]

    outs = []
    st_new = []
    for g in range(SSD_GROUPS):
        gs = slice(g * gw, (g + 1) * gw)
        c_g = cm[:, g * SSD_STATE:(g + 1) * SSD_STATE].astype(BF16)
        b_g = bm[:, g * SSD_STATE:(g + 1) * SSD_STATE].astype(BF16)
        scores = _dot_nt(c_g, b_g)
        xdt_g = xdt[:, gs].astype(BF16)
        p_parts = []
        x_parts = []
        for jj in range(hpg):
            j = g * hpg + jj
            diff = acs_b[:, j * LANES:(j + 1) * LANES] - acs_t[j:j + 1, :]
            seg = jnp.exp2(jnp.where(causal, diff, -jnp.inf))
            p_parts.append((scores * seg).astype(BF16))
            x_parts.append(xdt_g * head_b[jj])
        y_diag = _dot(jnp.concatenate(p_parts, axis=1), jnp.concatenate(x_parts, axis=0))
        st = st_ref[g]
        y_off = _dot(c_g, st.astype(BF16)) * eacs[:, gs]
        st_new.append(st * dec[:, gs] + _dot_tn(b_g, xdte[:, gs]))
        y = (y_diag + y_off + d_ref[:, gs] * xs[:, gs]) * zgate[:, gs]
        ms = jnp.mean(y * y, axis=-1, keepdims=True)
        outs.append((y * lax.rsqrt(ms + SSD_NORM_EPS) * nw_ref[:, gs]).astype(o_ref.dtype))
    for g in range(SSD_GROUPS):
        st_ref[g] = st_new[g]
    for g in range(SSD_GROUPS):
        o_ref[0, :, g * gw:(g + 1) * gw] = outs[g]


def _ssd_head_expanders():
    head_of_lane64 = jnp.arange(BRANCH_W) // SSD_HEAD
    head_of_lane128 = jnp.arange(SSD_HEADS * LANES) // LANES
    e64 = (jnp.arange(LANES)[:, None] == head_of_lane64[None, :]).astype(BF16)
    e128 = (jnp.arange(LANES)[:, None] == head_of_lane128[None, :]).astype(BF16)
    return e64, e128


def _ssd_branch(pb, layer, conv_w, conv_b, dt_bias, a_log, d_skip, norm_w, e64, e128):
    bsz, seq, cols = pb.shape
    pad16 = lambda t: jnp.pad(t, ((0, 0), (0, LANES - SSD_HEADS)))
    d_full = jnp.repeat(d_skip, SSD_HEAD, axis=1)
    const = lambda shape: pl.BlockSpec(shape, lambda b, c: (0, 0))
    blocks = (SSD_CHUNK * cols * 4 + (SSD_CONV + 1) * SSD_XBC * 4 + 4 * BRANCH_W * 4
              + e64.size * 2 + e128.size * 2 + SSD_CHUNK * BRANCH_W * 2)
    scratch = (HALO + SSD_CHUNK) * SSD_XBC * 4 + SSD_STATE * BRANCH_W * 4
    return pl.pallas_call(
        _ssd_kernel,
        grid=(bsz, seq // SSD_CHUNK),
        in_specs=[pl.BlockSpec((1, SSD_CHUNK, cols), lambda b, c: (b, c, 0)),
                  _layer_spec((SSD_CONV, SSD_XBC), layer), _layer_spec((1, SSD_XBC), layer),
                  _layer_spec((1, LANES), layer), _layer_spec((1, LANES), layer),
                  _layer_spec((1, BRANCH_W), layer), _layer_spec((1, BRANCH_W), layer),
                  const((LANES, BRANCH_W)), const((LANES, SSD_HEADS * LANES))],
        out_specs=pl.BlockSpec((1, SSD_CHUNK, BRANCH_W), lambda b, c: (b, c, 0)),
        out_shape=jax.ShapeDtypeStruct((bsz, seq, BRANCH_W), BF16),
        scratch_shapes=[pltpu.VMEM((HALO + SSD_CHUNK, SSD_XBC), F32),
                        pltpu.VMEM((SSD_GROUPS, SSD_STATE, BRANCH_W // SSD_GROUPS), F32)],
        compiler_params=_cparams(("parallel", "arbitrary"), blocks, scratch),
    )(pb, conv_w, _rows(conv_b), _rows(pad16(dt_bias)), _rows(pad16(a_log)), _rows(d_full),
      _rows(norm_w), e64, e128)


def _merge_out_kernel(oa_ref, ob_ref, pc_ref, cw_ref, ga_ref, gb_ref, gc_ref, wb_ref, wo_ref,
                      x_ref, nw_ref, *refs, tiles_per_seq):
    o_refs, ext_ref = refs[:-1], refs[-1]
    tm = pc_ref.shape[0]

    @pl.when(lax.rem(pl.program_id(0), tiles_per_seq) == 0)
    def _():
        ext_ref[0:HALO, :] = jnp.zeros((HALO, BRANCH_W), F32)

    u = (pc_ref[:, BRANCH_W:2 * BRANCH_W].astype(F32)
         * pc_ref[:, 2 * BRANCH_W:3 * BRANCH_W].astype(F32))
    ext_ref[HALO:HALO + tm, :] = u
    conv = cw_ref[SC_CONV - 1:SC_CONV, :] * u
    for kk in range(SC_CONV - 1):
        back = SC_CONV - 1 - kk
        conv = conv + cw_ref[kk:kk + 1, :] * ext_ref[pl.ds(HALO - back, tm), :]
    ext_ref[0:HALO, :] = u[tm - HALO:tm, :]
    oc = (pc_ref[:, 0:BRANCH_W].astype(F32) * conv
          * _silu(pc_ref[:, 3 * BRANCH_W:4 * BRANCH_W].astype(F32))).astype(BF16)

    acc = _sigmoid(ga_ref[...].astype(F32)) * _dot(oa_ref[...], wb_ref[0])
    acc = acc + _sigmoid(gb_ref[...].astype(F32)) * _dot(ob_ref[...], wb_ref[1])
    acc = acc + _sigmoid(gc_ref[...].astype(F32)) * _dot(oc, wb_ref[2])
    xn = x_ref[...] + _dot(acc.astype(BF16), wo_ref[...])
    ms = jnp.mean(xn * xn, axis=-1, keepdims=True)
    h_ref = o_refs[-1]
    h_ref[...] = (xn * lax.rsqrt(ms + NORM_EPS) * nw_ref[...]).astype(h_ref.dtype)
    if len(o_refs) == 2:
        o_refs[0][...] = xn


def _merge_out(oa, ob, pc, pg, layer, conv_w, wb, wo, x, nw, seq, norm_dtype, keep_residual, tm=256):
    m = oa.shape[0]
    d = D_MODEL
    assert seq % tm == 0
    row = lambda i: (i, 0)
    once = pl.Buffered(1)
    o_spec = pl.BlockSpec((tm, BRANCH_W), row)
    g_spec = lambda nb: pl.BlockSpec((tm, d), lambda i: (i, nb))
    norm_bytes = jnp.dtype(norm_dtype).itemsize
    blocks = (2 * tm * BRANCH_W * 2 + tm * SC_COLS * pc.dtype.itemsize + SC_CONV * BRANCH_W * 4
              + N_BRANCH * tm * d * pg.dtype.itemsize + tm * d * (4 + norm_bytes) + d * 4)
    out_specs = [pl.BlockSpec((tm, d), row)]
    out_shape = [jax.ShapeDtypeStruct((m, d), norm_dtype)]
    if keep_residual:
        blocks += tm * d * 4
        out_specs.insert(0, pl.BlockSpec((tm, d), row))
        out_shape.insert(0, jax.ShapeDtypeStruct((m, d), F32))
    resident = (wb.size * 2 + wo.size * 2 + tm * d * 6
                + (HALO + tm) * BRANCH_W * 4)
    return pl.pallas_call(
        functools.partial(_merge_out_kernel, tiles_per_seq=seq // tm),
        grid=(m // tm,),
        in_specs=[o_spec, o_spec, pl.BlockSpec((tm, SC_COLS), row),
                  _layer_spec((SC_CONV, BRANCH_W), layer), g_spec(0), g_spec(1), g_spec(2),
                  pl.BlockSpec((N_BRANCH, BRANCH_W, d), lambda i: (0, 0, 0), pipeline_mode=once),
                  pl.BlockSpec((d, d), lambda i: (0, 0), pipeline_mode=once),
                  pl.BlockSpec((tm, d), row),
                  pl.BlockSpec((1, d), lambda i: (0, 0))],
        out_specs=out_specs,
        out_shape=out_shape,
        scratch_shapes=[pltpu.VMEM((HALO + tm, BRANCH_W), F32)],
        compiler_params=_cparams(("arbitrary",), blocks, resident),
    )(oa, ob, pc, conv_w, pg, pg, pg, wb, wo, x, nw.reshape(1, d))


def _layer(x2, h, bsz, seq, next_norm_w, last, layer, p):
    m = bsz * seq
    wt = p["w_in_t"]
    c0, c1, c2 = RW_COLS, RW_COLS + SSD_COLS, RW_COLS + SSD_COLS + SC_COLS
    w_a = _cast_rows(wt, layer, 0, RW_COLS, RW_COLS, tr=1056)
    w_b = _cast_rows(wt, layer, c0, SSD_COLS, SSD_COLS_PAD, tr=640)
    w_c = _cast_rows(wt, layer, c1, SC_COLS, SC_COLS, tr=1024)
    w_g = _cast_rows(wt, layer, c2, GATE_COLS, GATE_COLS, tr=1024)

    pa = _matmul_shift(h, w_a, p["rw_mu"], layer, seq, tm=1024, tn=1408)
    pb = _matmul(h, w_b, F32, tm=512, tn=SSD_COLS_PAD)
    pc = _matmul(h, w_c, BF16, tm=1024, tn=2048)
    pg = _matmul(h, w_g, BF16, tm=1024, tn=2048)

    oa = _rwkv_branch(pa.reshape(bsz, seq, RW_COLS), layer, p["rw_w0"], p["rw_w2a2"], p["rw_a0"],
                      p["rw_kk"], p["rw_ka"], p["rw_rk"], p["rw_ln_w"], p["rw_ln_b"])
    ob = _ssd_branch(pb.reshape(bsz, seq, SSD_COLS_PAD), layer, p["ssd_conv_w"], p["ssd_conv_b"],
                     p["ssd_dt_bias"], p["ssd_a_log"], p["ssd_d"], p["ssd_norm_w"],
                     p["ssd_e64"], p["ssd_e128"])

    wb = _cast_bf16(p["w_branch"].reshape(-1, N_BRANCH * BRANCH_W, D_MODEL), layer)
    outs = _merge_out(oa.reshape(m, BRANCH_W), ob.reshape(m, BRANCH_W), pc, pg, layer,
                      p["sc_conv_w"], wb.reshape(N_BRANCH, BRANCH_W, D_MODEL),
                      _cast_bf16(p["w_out"], layer), x2, next_norm_w, seq,
                      F32 if last else BF16, keep_residual=not last)
    if last:
        return None, outs[0]
    return outs


def kernel(x, norm_w, w_in, rw_mu, rw_w0, rw_w2, rw_a0, rw_a2, rw_kk, rw_ka, rw_rk, rw_ln_w,
           rw_ln_b, ssd_conv_w, ssd_conv_b, ssd_dt_bias, ssd_a_log, ssd_d, ssd_norm_w,
           sc_conv_w, w_branch, w_out, final_norm_w):
    bsz, seq, d = x.shape
    depth = w_in.shape[0]
    zeros = jnp.zeros((depth, RW_RANK, BRANCH_W), F32)
    rw_w2a2 = jnp.concatenate([jnp.concatenate([rw_w2, zeros], axis=2),
                               jnp.concatenate([zeros, rw_a2], axis=2)], axis=1).astype(BF16)
    e64, e128 = _ssd_head_expanders()
    p = dict(w_in_t=jnp.transpose(w_in, (0, 2, 1)), w_branch=w_branch, w_out=w_out, rw_mu=rw_mu,
             rw_w0=rw_w0, rw_w2a2=rw_w2a2,
             rw_a0=rw_a0, rw_kk=rw_kk, rw_ka=rw_ka, rw_rk=rw_rk, rw_ln_w=rw_ln_w, rw_ln_b=rw_ln_b,
             ssd_conv_w=ssd_conv_w, ssd_conv_b=ssd_conv_b, ssd_dt_bias=ssd_dt_bias,
             ssd_a_log=ssd_a_log, ssd_d=ssd_d, ssd_norm_w=ssd_norm_w, ssd_e64=e64, ssd_e128=e128,
             sc_conv_w=sc_conv_w)
    x2 = x.reshape(bsz * seq, d)
    h = _rmsnorm(x2, norm_w[0], BF16)
    for l in range(depth):
        last = l == depth - 1
        next_norm_w = final_norm_w if last else norm_w[l + 1]
        x2, h = _layer(x2, h, bsz, seq, next_norm_w, last, l, p)
    return h.reshape(bsz, seq, d)
```

```python
import functools
import math

import jax
import jax.numpy as jnp
from jax import lax
from jax.experimental import pallas as pl
from jax.experimental.pallas import tpu as pltpu

F32 = jnp.float32
BF16 = jnp.bfloat16

D_MODEL = 2048
DEPTH = 2
BRANCH_W = D_MODEL // 2
N_BRANCH = 3
RW_HEAD = 64
RW_HEADS = BRANCH_W // RW_HEAD
RW_RANK = 64
RW_LN_EPS = 64e-5
RW_COLS = 4 * BRANCH_W + 2 * RW_RANK
SSD_HEAD = 64
SSD_HEADS = BRANCH_W // SSD_HEAD
SSD_GROUPS = 4
SSD_STATE = 128
SSD_CONV = 4
SSD_XBC = BRANCH_W + 2 * SSD_GROUPS * SSD_STATE
SSD_COLS = BRANCH_W + SSD_XBC + SSD_HEADS
SSD_NORM_EPS = 1e-5
SC_CONV = 3
SC_COLS = 4 * BRANCH_W
GATE_COLS = N_BRANCH * D_MODEL
NORM_EPS = 1e-6
LOG2E = math.log2(math.e)

LANES = 128
SSD_COLS_PAD = 3200
RW_CHUNK = 64
RW_STEP = 256
SSD_CHUNK = 128
SUBLANES = 8
HALO = SUBLANES
VMEM_PHYSICAL = 64 * 1024 * 1024
VMEM_INTERNAL = 8 * 1024 * 1024

assert RW_CHUNK == RW_HEAD and 2 * RW_HEAD == LANES


def _cparams(sem, block_bytes, scratch_bytes=0):
    limit = min(2 * block_bytes + scratch_bytes + VMEM_INTERNAL, VMEM_PHYSICAL - VMEM_INTERNAL)
    return pltpu.CompilerParams(dimension_semantics=sem, vmem_limit_bytes=limit)


def _sigmoid(x):
    return 1.0 / (1.0 + jnp.exp(-x))


def _silu(x):
    return x * _sigmoid(x)


def _softplus(x):
    return jnp.maximum(x, 0.0) + jnp.log1p(jnp.exp(-jnp.abs(x)))


def _dot(a, b):
    return jnp.dot(a, b, preferred_element_type=F32)


def _dot_nt(a, b):
    return lax.dot_general(a, b, (((1,), (1,)), ((), ())), preferred_element_type=F32)


def _dot_tn(a, b):
    return lax.dot_general(a, b, (((0,), (0,)), ((), ())), preferred_element_type=F32)


def _split3(x):
    hi = x.astype(BF16)
    r1 = x - hi.astype(F32)
    mid = r1.astype(BF16)
    lo = (r1 - mid.astype(F32)).astype(BF16)
    return hi, mid, lo


def _dot3(x, m):
    hi, mid, lo = _split3(x)
    return _dot(hi, m) + _dot(mid, m) + _dot(lo, m)


def _dot3_left(m, x):
    hi, mid, lo = _split3(x)
    return _dot(m, hi) + _dot(m, mid) + _dot(m, lo)


def _rmsnorm_kernel(x_ref, w_ref, o_ref):
    x = x_ref[...]
    ms = jnp.mean(x * x, axis=-1, keepdims=True)
    o_ref[...] = (x * lax.rsqrt(ms + NORM_EPS) * w_ref[...]).astype(o_ref.dtype)


def _rmsnorm(x, w, out_dtype, tm=512):
    m, d = x.shape
    blocks = tm * d * (4 + jnp.dtype(out_dtype).itemsize) + d * 4
    return pl.pallas_call(
        _rmsnorm_kernel,
        grid=(m // tm,),
        in_specs=[pl.BlockSpec((tm, d), lambda i: (i, 0)),
                  pl.BlockSpec((1, d), lambda i: (0, 0))],
        out_specs=pl.BlockSpec((tm, d), lambda i: (i, 0)),
        out_shape=jax.ShapeDtypeStruct((m, d), out_dtype),
        compiler_params=_cparams(("parallel",), blocks),
    )(x, w.reshape(1, d))


def _matmul_kernel(a_ref, bt_ref, o_ref):
    o_ref[...] = _dot_nt(a_ref[...], bt_ref[...]).astype(o_ref.dtype)


def _matmul(a, bt, out_dtype, tm, tn):
    m, k = a.shape
    n, _ = bt.shape
    blocks = tm * k * 2 + k * tn * 2 + tm * tn * jnp.dtype(out_dtype).itemsize
    return pl.pallas_call(
        _matmul_kernel,
        grid=(m // tm, n // tn),
        in_specs=[pl.BlockSpec((tm, k), lambda i, j: (i, 0)),
                  pl.BlockSpec((tn, k), lambda i, j: (j, 0))],
        out_specs=pl.BlockSpec((tm, tn), lambda i, j: (i, j)),
        out_shape=jax.ShapeDtypeStruct((m, n), out_dtype),
        compiler_params=_cparams(("parallel", "parallel"), blocks),
    )(a, bt)


def _matmul_shift_kernel(a_ref, b_ref, mu_ref, o_ref, carry_ref, *, tiles_per_seq):
    i = pl.program_id(0)
    j = pl.program_id(1)
    p = _dot_nt(a_ref[...], b_ref[...])
    tm = p.shape[0]
    prev_last = jnp.where(lax.rem(i, tiles_per_seq) == 0, 0.0, carry_ref[j, 0:1, :])
    row = lax.broadcasted_iota(jnp.int32, p.shape, 0)
    shifted = jnp.where(row == 0, prev_last, pltpu.roll(p, 1, axis=0))
    carry_ref[j, 0:1, :] = p[tm - 1:tm, :]
    o_ref[...] = p + mu_ref[...] * (shifted - p)


def _matmul_shift(a, b, mu, layer, seq, tm, tn):
    m, k = a.shape
    n, _ = b.shape
    assert seq % tm == 0
    whole = tn == n
    blocks = tm * k * 2 + (0 if whole else k * tn * 2) + tm * tn * 4 + tn * 4
    scratch = (n // tn) * HALO * tn * 4 + (k * tn * 2 if whole else 0)
    return pl.pallas_call(
        functools.partial(_matmul_shift_kernel, tiles_per_seq=seq // tm),
        grid=(m // tm, n // tn),
        in_specs=[pl.BlockSpec((tm, k), lambda i, j: (i, 0)),
                  pl.BlockSpec((tn, k), lambda i, j: (j, 0),
                               pipeline_mode=pl.Buffered(1) if whole else None),
                  pl.BlockSpec((None, 1, tn), lambda i, j: (layer, 0, j))],
        out_specs=pl.BlockSpec((tm, tn), lambda i, j: (i, j)),
        out_shape=jax.ShapeDtypeStruct((m, n), F32),
        scratch_shapes=[pltpu.VMEM((n // tn, HALO, tn), F32)],
        compiler_params=_cparams(("arbitrary", "arbitrary"), blocks, scratch),
    )(a, b, _rows(mu))


def _cast_rows_kernel(w_ref, o_ref, *, valid_rows):
    tr = o_ref.shape[0]
    row = pl.program_id(0) * tr + lax.broadcasted_iota(jnp.int32, o_ref.shape, 0)
    o_ref[...] = jnp.where(row < valid_rows, w_ref[0], 0.0).astype(o_ref.dtype)


def _cast_rows(wt, layer, start, size, size_pad, tr):
    _, _, k = wt.shape
    assert start % SUBLANES == 0 and tr % SUBLANES == 0 and size_pad % tr == 0
    return pl.pallas_call(
        functools.partial(_cast_rows_kernel, valid_rows=size),
        grid=(size_pad // tr,),
        in_specs=[pl.BlockSpec((pl.Element(1), pl.Element(tr), pl.Element(k)),
                               lambda i: (layer, pl.multiple_of(start + i * tr, SUBLANES), 0))],
        out_specs=pl.BlockSpec((tr, k), lambda i: (i, 0)),
        out_shape=jax.ShapeDtypeStruct((size_pad, k), BF16),
        compiler_params=_cparams(("parallel",), tr * k * 6),
    )(wt)


def _cast_kernel(x_ref, o_ref):
    o_ref[...] = x_ref[...].astype(o_ref.dtype)


def _cast_bf16(x, layer, tr=512):
    _, m, n = x.shape
    return pl.pallas_call(
        _cast_kernel,
        grid=(m // tr,),
        in_specs=[pl.BlockSpec((None, tr, n), lambda i: (layer, i, 0))],
        out_specs=pl.BlockSpec((tr, n), lambda i: (i, 0)),
        out_shape=jax.ShapeDtypeStruct((m, n), BF16),
        compiler_params=_cparams(("parallel",), tr * n * 6),
    )(x)


def _rwkv_kernel(pa_ref, w0_ref, w2a2_ref, a0_ref, kk_ref, ka_ref, rk_ref,
                 lnw_ref, lnb_ref, o_ref, s_ref):
    c = RW_CHUNK
    n = RW_HEAD
    pw = 2 * n
    t = RW_STEP
    nchunks = t // c

    @pl.when(pl.program_id(1) == 0)
    def _():
        s_ref[...] = jnp.zeros_like(s_ref)

    ps = pa_ref[0]

    r = ps[:, 0:BRANCH_W]
    k = ps[:, BRANCH_W:2 * BRANCH_W]
    v = ps[:, 2 * BRANCH_W:3 * BRANCH_W]
    gate = _silu(ps[:, 3 * BRANCH_W:4 * BRANCH_W])
    z = ps[:, 4 * BRANCH_W:4 * BRANCH_W + 2 * RW_RANK]
    zlane = lax.broadcasted_iota(jnp.int32, z.shape, 1)
    zz = jnp.where(zlane < RW_RANK, jnp.tanh(z), z).astype(BF16)
    proj = _dot(zz, w2a2_ref[...])
    ld = -_sigmoid(w0_ref[...] + proj[:, :BRANCH_W]) * math.exp(-0.5)
    a = _sigmoid(a0_ref[...] + proj[:, BRANCH_W:])
    kkv = k * kk_ref[...]
    k2 = k * (1.0 + (a - 1.0) * ka_ref[...])
    rkr = r * k2 * rk_ref[...]

    ti = lax.broadcasted_iota(jnp.int32, (t, t), 0)
    tj = lax.broadcasted_iota(jnp.int32, (t, t), 1)
    same_chunk = (ti // c) == (tj // c)
    tril = jnp.where(jnp.logical_and(same_chunk, ti >= tj), 1.0, 0.0).astype(BF16)
    ld_hi = ld.astype(BF16)
    ld_lo = (ld - ld_hi.astype(F32)).astype(BF16)
    cum = _dot(tril, ld_hi) + _dot(tril, ld_lo)
    e_cum = jnp.exp(cum)
    e_ncum = 1.0 / e_cum
    e_cumx = jnp.exp(cum - ld)

    ci = lax.broadcasted_iota(jnp.int32, (c, pw), 0)
    lane = lax.broadcasted_iota(jnp.int32, (c, pw), 1)
    cj = lane & (c - 1)
    lo = lane < n
    strict = ci > cj
    incl = ci >= cj
    eye = jnp.where(ci == cj, 1.0, 0.0)
    lo_b = jnp.where(lo, 1.0, 0.0).astype(BF16)
    hi_b = jnp.where(lo, 0.0, 1.0).astype(BF16)
    lvl_masks = []
    shift = 0
    while (1 << shift) < c:
        same_big = (ci >> (shift + 1)) == (cj >> (shift + 1))
        diff_small = (ci >> shift) != (cj >> shift)
        lvl_masks.append(jnp.logical_and(same_big, diff_small))
        shift += 1
    bi = lax.broadcasted_iota(jnp.int32, (pw, pw), 0)
    bj = lax.broadcasted_iota(jnp.int32, (pw, pw), 1)
    blockdiag = (bi >= n) == (bj >= n)
    br, bc = bi & (c - 1), bj & (c - 1)
    lvl_bd = []
    for shift in range(1, len(lvl_masks)):
        same_big = (br >> (shift + 1)) == (bc >> (shift + 1))
        diff_small = (br >> shift) != (bc >> shift)
        keep = jnp.logical_and(blockdiag, jnp.logical_and(same_big, diff_small))
        lvl_bd.append(jnp.where(keep, 1.0, 0.0).astype(BF16))

    def bd(y):
        return jnp.concatenate([y * lo_b, y * hi_b], axis=0)

    def half_sums(x):
        s_lo = jnp.sum(jnp.where(lo, x, 0.0), axis=-1, keepdims=True)
        s_hi = jnp.sum(jnp.where(lo, 0.0, x), axis=-1, keepdims=True)
        return jnp.where(lo, s_lo, s_hi)

    pairs = range(RW_HEADS // 2)
    sls = [slice(j * pw, (j + 1) * pw) for j in pairs]
    state = [s_ref[j] for j in pairs]
    outs = []
    for q in range(nchunks):
        rows = slice(q * c, (q + 1) * c)
        blk = lambda arr, sl: arr[rows, sl]
        g_c = jnp.exp(cum[(q + 1) * c - 1:(q + 1) * c, :])
        kh = [blk(kkv, sl) * jnp.minimum(lax.rsqrt(half_sums(blk(kkv, sl) * blk(kkv, sl))), 1e12)
              for sl in sls]
        b2 = [kh[j] * blk(a, sls[j]) for j in pairs]
        e_tot = [blk(e_ncum, sl) * g_c[:, sl] for sl in sls]
        kt = [(kh[j] * blk(e_cumx, sls[j])).astype(BF16) for j in pairs]
        bt = [(b2[j] * blk(e_ncum, sls[j])).astype(BF16) for j in pairs]
        kkt = [(blk(k2, sl) * blk(e_ncum, sl)).astype(BF16) for sl in sls]
        rt = [(blk(r, sl) * blk(e_cum, sl)).astype(BF16) for sl in sls]
        be_n = [(-(b2[j] * e_tot[j])).astype(BF16) for j in pairs]
        ke = [(blk(k2, sls[j]) * e_tot[j]).astype(BF16) for j in pairs]
        vb = [blk(v, sl).astype(BF16) for sl in sls]

        x2 = [jnp.concatenate([kt[j], rt[j]], axis=0) for j in pairs]
        rr = [jnp.concatenate([bt[j] * lo_b, bt[j] * hi_b, kkt[j] * lo_b, kkt[j] * hi_b], axis=0)
              for j in pairs]
        gall = [_dot_nt(x2[j], rr[j]) for j in pairs]
        l_m = [jnp.where(strict, gall[j][:c, :pw], 0.0) for j in pairs]
        akk = [jnp.where(strict, gall[j][:c, pw:], 0.0).astype(BF16) for j in pairs]
        arb_n = [jnp.where(incl, -gall[j][c:, :pw], 0.0).astype(BF16) for j in pairs]
        ark = [jnp.where(incl, gall[j][c:, pw:], 0.0).astype(BF16) for j in pairs]

        minv = [eye - jnp.where(lvl_masks[0], l_m[j], 0.0) for j in pairs]
        l_b = [l_m[j].astype(BF16) for j in pairs]
        l_t = [jnp.concatenate([l_b[j], l_b[j]], axis=0) for j in pairs]
        for m_bd in lvl_bd:
            mb = [minv[j].astype(BF16) for j in pairs]
            t1 = [_dot(mb[j], l_t[j] * m_bd).astype(BF16) for j in pairs]
            minv = [minv[j] - _dot(t1[j], bd(mb[j])) for j in pairs]
        mb = [minv[j].astype(BF16) for j in pairs]

        bdv = [bd(vb[j]) for j in pairs]
        av = [_dot(akk[j], bdv[j]) for j in pairs]
        s_b = [state[j].astype(BF16) for j in pairs]
        ks = [_dot_nt(x2[j], s_b[j]) for j in pairs]
        u_b = [_dot(mb[j], bd((ks[j][:c] + av[j]).astype(BF16))).astype(BF16) for j in pairs]
        y = [ks[j][c:] + _dot(jnp.concatenate([ark[j], arb_n[j]], axis=1),
                              jnp.concatenate([bdv[j], bd(u_b[j])], axis=0)) for j in pairs]
        state = [state[j] * g_c[:, sls[j]]
                 + jnp.where(blockdiag,
                             _dot_tn(jnp.concatenate([vb[j], u_b[j]], axis=0),
                                     jnp.concatenate([ke[j], be_n[j]], axis=0)), 0.0)
                 for j in pairs]

        for j in pairs:
            sl = sls[j]
            yc = y[j] - half_sums(y[j]) * (1.0 / n)
            var = half_sums(yc * yc) * (1.0 / n)
            yn = yc * lax.rsqrt(var + RW_LN_EPS) * lnw_ref[:, sl] + lnb_ref[:, sl]
            bonus = half_sums(blk(rkr, sl)) * blk(v, sl)
            outs.append(((yn + bonus) * blk(gate, sl)).astype(o_ref.dtype))
    for j in pairs:
        s_ref[j] = state[j]
    for q in range(nchunks):
        for j in pairs:
            o_ref[0, q * c:(q + 1) * c, sls[j]] = outs[q * len(sls) + j]


def _rows(p):
    return p.reshape(p.shape[0], 1, -1)


def _layer_spec(shape, layer):
    return pl.BlockSpec((None,) + tuple(shape), lambda *_: (layer,) + (0,) * len(shape))


def _rwkv_branch(pa, layer, w0, w2a2, a0, kk, ka, rk, lnw, lnb):
    bsz, seq, cols = pa.shape
    vec = _layer_spec((1, BRANCH_W), layer)
    blocks = RW_STEP * cols * 4 + 8 * BRANCH_W * 4 + w2a2[0].size * 2 + RW_STEP * BRANCH_W * 2
    scratch = (RW_HEADS // 2) * LANES * LANES * 4
    return pl.pallas_call(
        _rwkv_kernel,
        grid=(bsz, seq // RW_STEP),
        in_specs=[pl.BlockSpec((1, RW_STEP, cols), lambda b, c: (b, c, 0)),
                  vec, _layer_spec((2 * RW_RANK, 2 * BRANCH_W), layer),
                  vec, vec, vec, vec, vec, vec],
        out_specs=pl.BlockSpec((1, RW_STEP, BRANCH_W), lambda b, c: (b, c, 0)),
        out_shape=jax.ShapeDtypeStruct((bsz, seq, BRANCH_W), BF16),
        scratch_shapes=[pltpu.VMEM((RW_HEADS // 2, LANES, LANES), F32)],
        compiler_params=_cparams(("parallel", "arbitrary"), blocks, scratch),
    )(pa, _rows(w0), w2a2, _rows(a0), _rows(kk), _rows(ka), _rows(rk), _rows(lnw), _rows(lnb))


def _ssd_kernel(pb_ref, cw_ref, cb_ref, dtb_ref, alog_ref, d_ref, nw_ref, e64_ref, e128_ref,
                o_ref, ext_ref, st_ref):
    l = SSD_CHUNK
    gw = BRANCH_W // SSD_GROUPS
    hpg = SSD_HEADS // SSD_GROUPS

    @pl.when(pl.program_id(1) == 0)
    def _():
        ext_ref[0:HALO, :] = jnp.zeros((HALO, SSD_XBC), F32)
        st_ref[...] = jnp.zeros_like(st_ref)

    pb = pb_ref[0]
    zgate = _silu(pb[:, 0:BRANCH_W])
    xbc = pb[:, BRANCH_W:BRANCH_W + SSD_XBC]
    dts = pb[:, BRANCH_W + SSD_XBC:BRANCH_W + SSD_XBC + LANES]

    ext_ref[HALO:HALO + l, :] = xbc
    conv = cb_ref[...] + cw_ref[SSD_CONV - 1:SSD_CONV, :] * xbc
    for kk in range(SSD_CONV - 1):
        back = SSD_CONV - 1 - kk
        conv = conv + cw_ref[kk:kk + 1, :] * ext_ref[pl.ds(HALO - back, l), :]
    ext_ref[0:HALO, :] = xbc[l - HALO:l, :]
    xa = _silu(conv)
    xs = xa[:, 0:BRANCH_W]
    bm = xa[:, BRANCH_W:BRANCH_W + SSD_GROUPS * SSD_STATE]
    cm = xa[:, BRANCH_W + SSD_GROUPS * SSD_STATE:]

    dt16 = _softplus(dts + dtb_ref[...])
    da16 = dt16 * (-jnp.exp(alog_ref[...]))
    ri = lax.broadcasted_iota(jnp.int32, (l, l), 0)
    rj = lax.broadcasted_iota(jnp.int32, (l, l), 1)
    causal = ri >= rj
    tril = jnp.where(causal, 1.0, 0.0).astype(BF16)
    acs16 = _dot3_left(tril, da16 * LOG2E)
    acs_t = acs16.T
    e64 = e64_ref[...]
    dt_full = _dot3(dt16, e64)
    acs_full = _dot3(acs16, e64)
    acs_b = _dot3(acs16, e128_ref[...])
    tot_full = acs_full[l - 1:l, :]
    dte = jnp.exp2(tot_full - acs_full)
    eacs = jnp.exp2(acs_full)
    dec = jnp.exp2(tot_full)
    xdt = xs * dt_full
    xdte = (xdt * dte).astype(BF16)
    glane = lax.broadcasted_iota(jnp.int32, (l, gw), 1) // SSD_HEAD
    head_b = [jnp.where(glane == jj, 1.0, 0.0).astype(BF16) for jj in range(hpg)]

    outs = []
    st_new = []
    for g in range(SSD_GROUPS):
        gs = slice(g * gw, (g + 1) * gw)
        c_g = cm[:, g * SSD_STATE:(g + 1) * SSD_STATE].astype(BF16)
        b_g = bm[:, g * SSD_STATE:(g + 1) * SSD_STATE].astype(BF16)
        scores = _dot_nt(c_g, b_g)
        xdt_g = xdt[:, gs].astype(BF16)
        p_parts = []
        x_parts = []
        for jj in range(hpg):
            j = g * hpg + jj
            diff = acs_b[:, j * LANES:(j + 1) * LANES] - acs_t[j:j + 1, :]
            seg = jnp.exp2(jnp.where(causal, diff, -jnp.inf))
            p_parts.append((scores * seg).astype(BF16))
            x_parts.append(xdt_g * head_b[jj])
        y_diag = _dot(jnp.concatenate(p_parts, axis=1), jnp.concatenate(x_parts, axis=0))
        st = st_ref[g]
        y_off = _dot(c_g, st.astype(BF16)) * eacs[:, gs]
        st_new.append(st * dec[:, gs] + _dot_tn(b_g, xdte[:, gs]))
        y = (y_diag + y_off + d_ref[:, gs] * xs[:, gs]) * zgate[:, gs]
        ms = jnp.mean(y * y, axis=-1, keepdims=True)
        outs.append((y * lax.rsqrt(ms + SSD_NORM_EPS) * nw_ref[:, gs]).astype(o_ref.dtype))
    for g in range(SSD_GROUPS):
        st_ref[g] = st_new[g]
    for g in range(SSD_GROUPS):
        o_ref[0, :, g * gw:(g + 1) * gw] = outs[g]


def _ssd_head_expanders():
    head_of_lane64 = jnp.arange(BRANCH_W) // SSD_HEAD
    head_of_lane128 = jnp.arange(SSD_HEADS * LANES) // LANES
    e64 = (jnp.arange(LANES)[:, None] == head_of_lane64[None, :]).astype(BF16)
    e128 = (jnp.arange(LANES)[:, None] == head_of_lane128[None, :]).astype(BF16)
    return e64, e128


def _ssd_branch(pb, layer, conv_w, conv_b, dt_bias, a_log, d_skip, norm_w, e64, e128):
    bsz, seq, cols = pb.shape
    pad16 = lambda t: jnp.pad(t, ((0, 0), (0, LANES - SSD_HEADS)))
    d_full = jnp.repeat(d_skip, SSD_HEAD, axis=1)
    const = lambda shape: pl.BlockSpec(shape, lambda b, c: (0, 0))
    blocks = (SSD_CHUNK * cols * 4 + (SSD_CONV + 1) * SSD_XBC * 4 + 4 * BRANCH_W * 4
              + e64.size * 2 + e128.size * 2 + SSD_CHUNK * BRANCH_W * 2)
    scratch = (HALO + SSD_CHUNK) * SSD_XBC * 4 + SSD_STATE * BRANCH_W * 4
    return pl.pallas_call(
        _ssd_kernel,
        grid=(bsz, seq // SSD_CHUNK),
        in_specs=[pl.BlockSpec((1, SSD_CHUNK, cols), lambda b, c: (b, c, 0)),
                  _layer_spec((SSD_CONV, SSD_XBC), layer), _layer_spec((1, SSD_XBC), layer),
                  _layer_spec((1, LANES), layer), _layer_spec((1, LANES), layer),
                  _layer_spec((1, BRANCH_W), layer), _layer_spec((1, BRANCH_W), layer),
                  const((LANES, BRANCH_W)), const((LANES, SSD_HEADS * LANES))],
        out_specs=pl.BlockSpec((1, SSD_CHUNK, BRANCH_W), lambda b, c: (b, c, 0)),
        out_shape=jax.ShapeDtypeStruct((bsz, seq, BRANCH_W), BF16),
        scratch_shapes=[pltpu.VMEM((HALO + SSD_CHUNK, SSD_XBC), F32),
                        pltpu.VMEM((SSD_GROUPS, SSD_STATE, BRANCH_W // SSD_GROUPS), F32)],
        compiler_params=_cparams(("parallel", "arbitrary"), blocks, scratch),
    )(pb, conv_w, _rows(conv_b), _rows(pad16(dt_bias)), _rows(pad16(a_log)), _rows(d_full),
      _rows(norm_w), e64, e128)


def _merge_out_kernel(oa_ref, ob_ref, pc_ref, cw_ref, ga_ref, gb_ref, gc_ref, wb_ref, wo_ref,
                      x_ref, nw_ref, *refs, tiles_per_seq):
    o_refs, ext_ref = refs[:-1], refs[-1]
    tm = pc_ref.shape[0]

    @pl.when(lax.rem(pl.program_id(0), tiles_per_seq) == 0)
    def _():
        ext_ref[0:HALO, :] = jnp.zeros((HALO, BRANCH_W), F32)

    u = (pc_ref[:, BRANCH_W:2 * BRANCH_W].astype(F32)
         * pc_ref[:, 2 * BRANCH_W:3 * BRANCH_W].astype(F32))
    ext_ref[HALO:HALO + tm, :] = u
    conv = cw_ref[SC_CONV - 1:SC_CONV, :] * u
    for kk in range(SC_CONV - 1):
        back = SC_CONV - 1 - kk
        conv = conv + cw_ref[kk:kk + 1, :] * ext_ref[pl.ds(HALO - back, tm), :]
    ext_ref[0:HALO, :] = u[tm - HALO:tm, :]
    oc = (pc_ref[:, 0:BRANCH_W].astype(F32) * conv
          * _silu(pc_ref[:, 3 * BRANCH_W:4 * BRANCH_W].astype(F32))).astype(BF16)

    acc = _sigmoid(ga_ref[...].astype(F32)) * _dot(oa_ref[...], wb_ref[0])
    acc = acc + _sigmoid(gb_ref[...].astype(F32)) * _dot(ob_ref[...], wb_ref[1])
    acc = acc + _sigmoid(gc_ref[...].astype(F32)) * _dot(oc, wb_ref[2])
    xn = x_ref[...] + _dot(acc.astype(BF16), wo_ref[...])
    ms = jnp.mean(xn * xn, axis=-1, keepdims=True)
    h_ref = o_refs[-1]
    h_ref[...] = (xn * lax.rsqrt(ms + NORM_EPS) * nw_ref[...]).astype(h_ref.dtype)
    if len(o_refs) == 2:
        o_refs[0][...] = xn


def _merge_out(oa, ob, pcg, layer, conv_w, wb, wo, x, nw, seq, norm_dtype, keep_residual, tm=256):
    m = oa.shape[0]
    d = D_MODEL
    assert seq % tm == 0
    row = lambda i: (i, 0)
    once = pl.Buffered(1)
    o_spec = pl.BlockSpec((tm, BRANCH_W), row)
    g_spec = lambda nb: pl.BlockSpec((tm, d), lambda i: (i, SC_COLS // d + nb))
    norm_bytes = jnp.dtype(norm_dtype).itemsize
    blocks = (2 * tm * BRANCH_W * 2 + tm * SC_COLS * pcg.dtype.itemsize + SC_CONV * BRANCH_W * 4
              + N_BRANCH * tm * d * pcg.dtype.itemsize + tm * d * (4 + norm_bytes) + d * 4)
    out_specs = [pl.BlockSpec((tm, d), row)]
    out_shape = [jax.ShapeDtypeStruct((m, d), norm_dtype)]
    if keep_residual:
        blocks += tm * d * 4
        out_specs.insert(0, pl.BlockSpec((tm, d), row))
        out_shape.insert(0, jax.ShapeDtypeStruct((m, d), F32))
    resident = (wb.size * 2 + wo.size * 2 + tm * d * 6
                + (HALO + tm) * BRANCH_W * 4)
    return pl.pallas_call(
        functools.partial(_merge_out_kernel, tiles_per_seq=seq // tm),
        grid=(m // tm,),
        in_specs=[o_spec, o_spec, pl.BlockSpec((tm, SC_COLS), row),
                  _layer_spec((SC_CONV, BRANCH_W), layer), g_spec(0), g_spec(1), g_spec(2),
                  pl.BlockSpec((N_BRANCH, BRANCH_W, d), lambda i: (0, 0, 0), pipeline_mode=once),
                  pl.BlockSpec((d, d), lambda i: (0, 0), pipeline_mode=once),
                  pl.BlockSpec((tm, d), row),
                  pl.BlockSpec((1, d), lambda i: (0, 0))],
        out_specs=out_specs,
        out_shape=out_shape,
        scratch_shapes=[pltpu.VMEM((HALO + tm, BRANCH_W), F32)],
        compiler_params=_cparams(("arbitrary",), blocks, resident),
    )(oa, ob, pcg, conv_w, pcg, pcg, pcg, wb, wo, x, nw.reshape(1, d))


def _layer(x2, h, bsz, seq, next_norm_w, last, layer, p):
    m = bsz * seq
    wt = p["w_in_t"]
    c0, c1 = RW_COLS, RW_COLS + SSD_COLS
    w_a = _cast_rows(wt, layer, 0, RW_COLS, RW_COLS, tr=1056)
    w_b = _cast_rows(wt, layer, c0, SSD_COLS, SSD_COLS_PAD, tr=640)
    w_cg = _cast_rows(wt, layer, c1, SC_COLS + GATE_COLS, SC_COLS + GATE_COLS, tr=1024)

    pa = _matmul_shift(h, w_a, p["rw_mu"], layer, seq, tm=512, tn=RW_COLS)
    pb = _matmul(h, w_b, F32, tm=512, tn=SSD_COLS_PAD)
    pcg = _matmul(h, w_cg, BF16, tm=1024, tn=2048)

    oa = _rwkv_branch(pa.reshape(bsz, seq, RW_COLS), layer, p["rw_w0"], p["rw_w2a2"], p["rw_a0"],
                      p["rw_kk"], p["rw_ka"], p["rw_rk"], p["rw_ln_w"], p["rw_ln_b"])
    ob = _ssd_branch(pb.reshape(bsz, seq, SSD_COLS_PAD), layer, p["ssd_conv_w"], p["ssd_conv_b"],
                     p["ssd_dt_bias"], p["ssd_a_log"], p["ssd_d"], p["ssd_norm_w"],
                     p["ssd_e64"], p["ssd_e128"])

    wb = _cast_bf16(p["w_branch"].reshape(-1, N_BRANCH * BRANCH_W, D_MODEL), layer)
    outs = _merge_out(oa.reshape(m, BRANCH_W), ob.reshape(m, BRANCH_W), pcg, layer,
                      p["sc_conv_w"], wb.reshape(N_BRANCH, BRANCH_W, D_MODEL),
                      _cast_bf16(p["w_out"], layer), x2, next_norm_w, seq,
                      F32 if last else BF16, keep_residual=not last)
    if last:
        return None, outs[0]
    return outs


def kernel(x, norm_w, w_in, rw_mu, rw_w0, rw_w2, rw_a0, rw_a2, rw_kk, rw_ka, rw_rk, rw_ln_w,
           rw_ln_b, ssd_conv_w, ssd_conv_b, ssd_dt_bias, ssd_a_log, ssd_d, ssd_norm_w,
           sc_conv_w, w_branch, w_out, final_norm_w):
    bsz, seq, d = x.shape
    depth = w_in.shape[0]
    zeros = jnp.zeros((depth, RW_RANK, BRANCH_W), F32)
    rw_w2a2 = jnp.concatenate([jnp.concatenate([rw_w2, zeros], axis=2),
                               jnp.concatenate([zeros, rw_a2], axis=2)], axis=1).astype(BF16)
    e64, e128 = _ssd_head_expanders()
    p = dict(w_in_t=jnp.transpose(w_in, (0, 2, 1)), w_branch=w_branch, w_out=w_out, rw_mu=rw_mu,
             rw_w0=rw_w0, rw_w2a2=rw_w2a2,
             rw_a0=rw_a0, rw_kk=rw_kk, rw_ka=rw_ka, rw_rk=rw_rk, rw_ln_w=rw_ln_w, rw_ln_b=rw_ln_b,
             ssd_conv_w=ssd_conv_w, ssd_conv_b=ssd_conv_b, ssd_dt_bias=ssd_dt_bias,
             ssd_a_log=ssd_a_log, ssd_d=ssd_d, ssd_norm_w=ssd_norm_w, ssd_e64=e64, ssd_e128=e128,
             sc_conv_w=sc_conv_w)
    x2 = x.reshape(bsz * seq, d)
    h = _rmsnorm(x2, norm_w[0], BF16)
    for l in range(depth):
        last = l == depth - 1
        next_norm_w = final_norm_w if last else norm_w[l + 1]
        x2, h = _layer(x2, h, bsz, seq, next_norm_w, last, l, p)
    return h.reshape(bsz, seq, d)
```

```python
import functools
import math

import jax
import jax.numpy as jnp
from jax import lax
from jax.experimental import pallas as pl
from jax.experimental.pallas import tpu as pltpu

F32 = jnp.float32
BF16 = jnp.bfloat16

D_MODEL = 2048
DEPTH = 2
BRANCH_W = D_MODEL // 2
N_BRANCH = 3
RW_HEAD = 64
RW_HEADS = BRANCH_W // RW_HEAD
RW_RANK = 64
RW_LN_EPS = 64e-5
RW_COLS = 4 * BRANCH_W + 2 * RW_RANK
SSD_HEAD = 64
SSD_HEADS = BRANCH_W // SSD_HEAD
SSD_GROUPS = 4
SSD_STATE = 128
SSD_CONV = 4
SSD_XBC = BRANCH_W + 2 * SSD_GROUPS * SSD_STATE
SSD_COLS = BRANCH_W + SSD_XBC + SSD_HEADS
SSD_NORM_EPS = 1e-5
SC_CONV = 3
SC_COLS = 4 * BRANCH_W
GATE_COLS = N_BRANCH * D_MODEL
NORM_EPS = 1e-6
LOG2E = math.log2(math.e)

LANES = 128
SSD_COLS_PAD = 3200
RW_CHUNK = 64
RW_STEP = 256
SSD_CHUNK = 128
SUBLANES = 8
HALO = SUBLANES
VMEM_PHYSICAL = 64 * 1024 * 1024
VMEM_INTERNAL = 8 * 1024 * 1024

assert RW_CHUNK == RW_HEAD and 2 * RW_HEAD == LANES


def _cparams(sem, block_bytes, scratch_bytes=0):
    limit = min(2 * block_bytes + scratch_bytes + VMEM_INTERNAL, VMEM_PHYSICAL - VMEM_INTERNAL)
    return pltpu.CompilerParams(dimension_semantics=sem, vmem_limit_bytes=limit)


def _sigmoid(x):
    return 1.0 / (1.0 + jnp.exp(-x))


def _silu(x):
    return x * _sigmoid(x)


def _softplus(x):
    return jnp.maximum(x, 0.0) + jnp.log1p(jnp.exp(-jnp.abs(x)))


def _dot(a, b):
    return jnp.dot(a, b, preferred_element_type=F32)


def _dot_nt(a, b):
    return lax.dot_general(a, b, (((1,), (1,)), ((), ())), preferred_element_type=F32)


def _dot_tn(a, b):
    return lax.dot_general(a, b, (((0,), (0,)), ((), ())), preferred_element_type=F32)


def _split3(x):
    hi = x.astype(BF16)
    r1 = x - hi.astype(F32)
    mid = r1.astype(BF16)
    lo = (r1 - mid.astype(F32)).astype(BF16)
    return hi, mid, lo


def _dot3(x, m):
    hi, mid, lo = _split3(x)
    return _dot(hi, m) + _dot(mid, m) + _dot(lo, m)


def _dot3_left(m, x):
    hi, mid, lo = _split3(x)
    return _dot(m, hi) + _dot(m, mid) + _dot(m, lo)


def _rmsnorm_kernel(x_ref, w_ref, o_ref):
    x = x_ref[...]
    ms = jnp.mean(x * x, axis=-1, keepdims=True)
    o_ref[...] = (x * lax.rsqrt(ms + NORM_EPS) * w_ref[...]).astype(o_ref.dtype)


def _rmsnorm(x, w, out_dtype, tm=512):
    m, d = x.shape
    blocks = tm * d * (4 + jnp.dtype(out_dtype).itemsize) + d * 4
    return pl.pallas_call(
        _rmsnorm_kernel,
        grid=(m // tm,),
        in_specs=[pl.BlockSpec((tm, d), lambda i: (i, 0)),
                  pl.BlockSpec((1, d), lambda i: (0, 0))],
        out_specs=pl.BlockSpec((tm, d), lambda i: (i, 0)),
        out_shape=jax.ShapeDtypeStruct((m, d), out_dtype),
        compiler_params=_cparams(("parallel",), blocks),
    )(x, w.reshape(1, d))


def _matmul_kernel(a_ref, bt_ref, o_ref):
    o_ref[...] = _dot_nt(a_ref[...], bt_ref[...]).astype(o_ref.dtype)


def _matmul(a, bt, out_dtype, tm, tn):
    m, k = a.shape
    n, _ = bt.shape
    blocks = tm * k * 2 + k * tn * 2 + tm * tn * jnp.dtype(out_dtype).itemsize
    return pl.pallas_call(
        _matmul_kernel,
        grid=(m // tm, n // tn),
        in_specs=[pl.BlockSpec((tm, k), lambda i, j: (i, 0)),
                  pl.BlockSpec((tn, k), lambda i, j: (j, 0))],
        out_specs=pl.BlockSpec((tm, tn), lambda i, j: (i, j)),
        out_shape=jax.ShapeDtypeStruct((m, n), out_dtype),
        compiler_params=_cparams(("parallel", "parallel"), blocks),
    )(a, bt)


def _matmul_shift_kernel(a_ref, b_ref, mu_ref, o_ref, carry_ref, *, tiles_per_seq):
    i = pl.program_id(0)
    j = pl.program_id(1)
    p = _dot_nt(a_ref[...], b_ref[...])
    tm = p.shape[0]
    prev_last = jnp.where(lax.rem(i, tiles_per_seq) == 0, 0.0, carry_ref[j, 0:1, :])
    row = lax.broadcasted_iota(jnp.int32, p.shape, 0)
    shifted = jnp.where(row == 0, prev_last, pltpu.roll(p, 1, axis=0))
    carry_ref[j, 0:1, :] = p[tm - 1:tm, :]
    o_ref[...] = p + mu_ref[...] * (shifted - p)


def _matmul_shift(a, b, mu, layer, seq, tm, tn):
    m, k = a.shape
    n, _ = b.shape
    assert seq % tm == 0
    whole = tn == n
    blocks = tm * k * 2 + (0 if whole else k * tn * 2) + tm * tn * 4 + tn * 4
    scratch = (n // tn) * HALO * tn * 4 + (k * tn * 2 if whole else 0)
    return pl.pallas_call(
        functools.partial(_matmul_shift_kernel, tiles_per_seq=seq // tm),
        grid=(m // tm, n // tn),
        in_specs=[pl.BlockSpec((tm, k), lambda i, j: (i, 0)),
                  pl.BlockSpec((tn, k), lambda i, j: (j, 0),
                               pipeline_mode=pl.Buffered(1) if whole else None),
                  pl.BlockSpec((None, 1, tn), lambda i, j: (layer, 0, j))],
        out_specs=pl.BlockSpec((tm, tn), lambda i, j: (i, j)),
        out_shape=jax.ShapeDtypeStruct((m, n), F32),
        scratch_shapes=[pltpu.VMEM((n // tn, HALO, tn), F32)],
        compiler_params=_cparams(("arbitrary", "arbitrary"), blocks, scratch),
    )(a, b, _rows(mu))


def _cast_rows_kernel(w_ref, o_ref, *, valid_rows):
    tr = o_ref.shape[0]
    row = pl.program_id(0) * tr + lax.broadcasted_iota(jnp.int32, o_ref.shape, 0)
    o_ref[...] = jnp.where(row < valid_rows, w_ref[0], 0.0).astype(o_ref.dtype)


def _cast_rows(wt, layer, start, size, size_pad, tr):
    _, _, k = wt.shape
    assert start % SUBLANES == 0 and tr % SUBLANES == 0 and size_pad % tr == 0
    return pl.pallas_call(
        functools.partial(_cast_rows_kernel, valid_rows=size),
        grid=(size_pad // tr,),
        in_specs=[pl.BlockSpec((pl.Element(1), pl.Element(tr), pl.Element(k)),
                               lambda i: (layer, pl.multiple_of(start + i * tr, SUBLANES), 0))],
        out_specs=pl.BlockSpec((tr, k), lambda i: (i, 0)),
        out_shape=jax.ShapeDtypeStruct((size_pad, k), BF16),
        compiler_params=_cparams(("parallel",), tr * k * 6),
    )(wt)


def _cast_kernel(x_ref, o_ref):
    o_ref[...] = x_ref[...].astype(o_ref.dtype)


def _cast_bf16(x, layer, tr=512):
    _, m, n = x.shape
    return pl.pallas_call(
        _cast_kernel,
        grid=(m // tr,),
        in_specs=[pl.BlockSpec((None, tr, n), lambda i: (layer, i, 0))],
        out_specs=pl.BlockSpec((tr, n), lambda i: (i, 0)),
        out_shape=jax.ShapeDtypeStruct((m, n), BF16),
        compiler_params=_cparams(("parallel",), tr * n * 6),
    )(x)


def _rwkv_kernel(pa_ref, w0_ref, w2a2_ref, a0_ref, kk_ref, ka_ref, rk_ref,
                 lnw_ref, lnb_ref, o_ref, s_ref):
    c = RW_CHUNK
    n = RW_HEAD
    pw = 2 * n
    t = RW_STEP
    nchunks = t // c

    @pl.when(pl.program_id(1) == 0)
    def _():
        s_ref[...] = jnp.zeros_like(s_ref)

    ps = pa_ref[0]

    r = ps[:, 0:BRANCH_W]
    k = ps[:, BRANCH_W:2 * BRANCH_W]
    v = ps[:, 2 * BRANCH_W:3 * BRANCH_W]
    gate = _silu(ps[:, 3 * BRANCH_W:4 * BRANCH_W])
    z = ps[:, 4 * BRANCH_W:4 * BRANCH_W + 2 * RW_RANK]
    zlane = lax.broadcasted_iota(jnp.int32, z.shape, 1)
    zz = jnp.where(zlane < RW_RANK, jnp.tanh(z), z).astype(BF16)
    proj = _dot(zz, w2a2_ref[...])
    ld = -_sigmoid(w0_ref[...] + proj[:, :BRANCH_W]) * math.exp(-0.5)
    a = _sigmoid(a0_ref[...] + proj[:, BRANCH_W:])
    kkv = k * kk_ref[...]
    k2 = k * (1.0 + (a - 1.0) * ka_ref[...])
    rkr = r * k2 * rk_ref[...]

    ti = lax.broadcasted_iota(jnp.int32, (t, t), 0)
    tj = lax.broadcasted_iota(jnp.int32, (t, t), 1)
    same_chunk = (ti // c) == (tj // c)
    tril = jnp.where(jnp.logical_and(same_chunk, ti >= tj), 1.0, 0.0).astype(BF16)
    ld_hi = ld.astype(BF16)
    ld_lo = (ld - ld_hi.astype(F32)).astype(BF16)
    cum = _dot(tril, ld_hi) + _dot(tril, ld_lo)
    e_cum = jnp.exp(cum)
    e_ncum = 1.0 / e_cum
    e_cumx = jnp.exp(cum - ld)

    ci = lax.broadcasted_iota(jnp.int32, (c, pw), 0)
    lane = lax.broadcasted_iota(jnp.int32, (c, pw), 1)
    cj = lane & (c - 1)
    lo = lane < n
    strict = ci > cj
    incl = ci >= cj
    eye = jnp.where(ci == cj, 1.0, 0.0)
    lo_b = jnp.where(lo, 1.0, 0.0).astype(BF16)
    hi_b = jnp.where(lo, 0.0, 1.0).astype(BF16)
    lvl_masks = []
    shift = 0
    while (1 << shift) < c:
        same_big = (ci >> (shift + 1)) == (cj >> (shift + 1))
        diff_small = (ci >> shift) != (cj >> shift)
        lvl_masks.append(jnp.logical_and(same_big, diff_small))
        shift += 1
    bi = lax.broadcasted_iota(jnp.int32, (pw, pw), 0)
    bj = lax.broadcasted_iota(jnp.int32, (pw, pw), 1)
    blockdiag = (bi >= n) == (bj >= n)
    br, bc = bi & (c - 1), bj & (c - 1)
    lvl_bd = []
    for shift in range(1, len(lvl_masks)):
        same_big = (br >> (shift + 1)) == (bc >> (shift + 1))
        diff_small = (br >> shift) != (bc >> shift)
        keep = jnp.logical_and(blockdiag, jnp.logical_and(same_big, diff_small))
        lvl_bd.append(jnp.where(keep, 1.0, 0.0).astype(BF16))

    ones_bd = jnp.where(blockdiag, 1.0, 0.0).astype(BF16)

    def bd(y):
        return jnp.concatenate([y * lo_b, y * hi_b], axis=0)

    def half_sums(x):
        s_lo = jnp.sum(jnp.where(lo, x, 0.0), axis=-1, keepdims=True)
        s_hi = jnp.sum(jnp.where(lo, 0.0, x), axis=-1, keepdims=True)
        return jnp.where(lo, s_lo, s_hi)

    pairs = range(RW_HEADS // 2)
    sls = [slice(j * pw, (j + 1) * pw) for j in pairs]
    state = [s_ref[j] for j in pairs]
    outs = []
    for q in range(nchunks):
        rows = slice(q * c, (q + 1) * c)
        blk = lambda arr, sl: arr[rows, sl]
        g_c = jnp.exp(cum[(q + 1) * c - 1:(q + 1) * c, :])
        kh = [blk(kkv, sl) * jnp.minimum(lax.rsqrt(half_sums(blk(kkv, sl) * blk(kkv, sl))), 1e12)
              for sl in sls]
        b2 = [kh[j] * blk(a, sls[j]) for j in pairs]
        e_tot = [blk(e_ncum, sl) * g_c[:, sl] for sl in sls]
        kt = [(kh[j] * blk(e_cumx, sls[j])).astype(BF16) for j in pairs]
        bt = [(b2[j] * blk(e_ncum, sls[j])).astype(BF16) for j in pairs]
        kkt = [(blk(k2, sl) * blk(e_ncum, sl)).astype(BF16) for sl in sls]
        rt = [(blk(r, sl) * blk(e_cum, sl)).astype(BF16) for sl in sls]
        be_n = [(-(b2[j] * e_tot[j])).astype(BF16) for j in pairs]
        ke = [(blk(k2, sls[j]) * e_tot[j]).astype(BF16) for j in pairs]
        vb = [blk(v, sl).astype(BF16) for sl in sls]

        x2 = [jnp.concatenate([kt[j], rt[j]], axis=0) for j in pairs]
        rr = [jnp.concatenate([bt[j] * lo_b, bt[j] * hi_b, kkt[j] * lo_b, kkt[j] * hi_b], axis=0)
              for j in pairs]
        gall = [_dot_nt(x2[j], rr[j]) for j in pairs]
        l_m = [jnp.where(strict, gall[j][:c, :pw], 0.0) for j in pairs]
        akk = [jnp.where(strict, gall[j][:c, pw:], 0.0).astype(BF16) for j in pairs]
        arb_n = [jnp.where(incl, -gall[j][c:, :pw], 0.0).astype(BF16) for j in pairs]
        ark = [jnp.where(incl, gall[j][c:, pw:], 0.0).astype(BF16) for j in pairs]

        minv = [eye - jnp.where(lvl_masks[0], l_m[j], 0.0) for j in pairs]
        l_b = [l_m[j].astype(BF16) for j in pairs]
        l_t = [jnp.concatenate([l_b[j], l_b[j]], axis=0) for j in pairs]
        for m_bd in lvl_bd:
            mb = [minv[j].astype(BF16) for j in pairs]
            t1 = [_dot(mb[j], l_t[j] * m_bd).astype(BF16) for j in pairs]
            minv = [minv[j] - _dot(t1[j], bd(mb[j])) for j in pairs]
        mb = [minv[j].astype(BF16) for j in pairs]

        bdv = [bd(vb[j]) for j in pairs]
        av = [_dot(akk[j], bdv[j]) for j in pairs]
        s_b = [state[j].astype(BF16) for j in pairs]
        ks = [_dot_nt(x2[j], s_b[j]) for j in pairs]
        u_b = [_dot(mb[j], bd((ks[j][:c] + av[j]).astype(BF16))).astype(BF16) for j in pairs]
        y = [ks[j][c:] + _dot(jnp.concatenate([ark[j], arb_n[j]], axis=1),
                              jnp.concatenate([bdv[j], bd(u_b[j])], axis=0)) for j in pairs]
        state = [state[j] * g_c[:, sls[j]]
                 + jnp.where(blockdiag,
                             _dot_tn(jnp.concatenate([vb[j], u_b[j]], axis=0),
                                     jnp.concatenate([ke[j], be_n[j]], axis=0)), 0.0)
                 for j in pairs]

        for j in pairs:
            sl = sls[j]
            yc = y[j] - _dot(y[j].astype(BF16), ones_bd) * (1.0 / n)
            var = _dot((yc * yc).astype(BF16), ones_bd) * (1.0 / n)
            yn = yc * lax.rsqrt(var + RW_LN_EPS) * lnw_ref[:, sl] + lnb_ref[:, sl]
            bonus = half_sums(blk(rkr, sl)) * blk(v, sl)
            outs.append(((yn + bonus) * blk(gate, sl)).astype(o_ref.dtype))
    for j in pairs:
        s_ref[j] = state[j]
    for q in range(nchunks):
        for j in pairs:
            o_ref[0, q * c:(q + 1) * c, sls[j]] = outs[q * len(sls) + j]


def _rows(p):
    return p.reshape(p.shape[0], 1, -1)


def _layer_spec(shape, layer):
    return pl.BlockSpec((None,) + tuple(shape), lambda *_: (layer,) + (0,) * len(shape))


def _rwkv_branch(pa, layer, w0, w2a2, a0, kk, ka, rk, lnw, lnb):
    bsz, seq, cols = pa.shape
    vec = _layer_spec((1, BRANCH_W), layer)
    blocks = RW_STEP * cols * 4 + 8 * BRANCH_W * 4 + w2a2[0].size * 2 + RW_STEP * BRANCH_W * 2
    scratch = (RW_HEADS // 2) * LANES * LANES * 4
    return pl.pallas_call(
        _rwkv_kernel,
        grid=(bsz, seq // RW_STEP),
        in_specs=[pl.BlockSpec((1, RW_STEP, cols), lambda b, c: (b, c, 0)),
                  vec, _layer_spec((2 * RW_RANK, 2 * BRANCH_W), layer),
                  vec, vec, vec, vec, vec, vec],
        out_specs=pl.BlockSpec((1, RW_STEP, BRANCH_W), lambda b, c: (b, c, 0)),
        out_shape=jax.ShapeDtypeStruct((bsz, seq, BRANCH_W), BF16),
        scratch_shapes=[pltpu.VMEM((RW_HEADS // 2, LANES, LANES), F32)],
        compiler_params=_cparams(("parallel", "arbitrary"), blocks, scratch),
    )(pa, _rows(w0), w2a2, _rows(a0), _rows(kk), _rows(ka), _rows(rk), _rows(lnw), _rows(lnb))


def _ssd_kernel(pb_ref, cw_ref, cb_ref, dtb_ref, alog_ref, d_ref, nw_ref, e64_ref, e128_ref,
                o_ref, ext_ref, st_ref):
    l = SSD_CHUNK
    gw = BRANCH_W // SSD_GROUPS
    hpg = SSD_HEADS // SSD_GROUPS

    @pl.when(pl.program_id(1) == 0)
    def _():
        ext_ref[0:HALO, :] = jnp.zeros((HALO, SSD_XBC), F32)
        st_ref[...] = jnp.zeros_like(st_ref)

    pb = pb_ref[0]
    zgate = _silu(pb[:, 0:BRANCH_W])
    xbc = pb[:, BRANCH_W:BRANCH_W + SSD_XBC]
    dts = pb[:, BRANCH_W + SSD_XBC:BRANCH_W + SSD_XBC + LANES]

    ext_ref[HALO:HALO + l, :] = xbc
    conv = cb_ref[...] + cw_ref[SSD_CONV - 1:SSD_CONV, :] * xbc
    for kk in range(SSD_CONV - 1):
        back = SSD_CONV - 1 - kk
        conv = conv + cw_ref[kk:kk + 1, :] * ext_ref[pl.ds(HALO - back, l), :]
    ext_ref[0:HALO, :] = xbc[l - HALO:l, :]
    xa = _silu(conv)
    xs = xa[:, 0:BRANCH_W]
    bm = xa[:, BRANCH_W:BRANCH_W + SSD_GROUPS * SSD_STATE]
    cm = xa[:, BRANCH_W + SSD_GROUPS * SSD_STATE:]

    dt16 = _softplus(dts + dtb_ref[...])
    da16 = dt16 * (-jnp.exp(alog_ref[...]))
    ri = lax.broadcasted_iota(jnp.int32, (l, l), 0)
    rj = lax.broadcasted_iota(jnp.int32, (l, l), 1)
    causal = ri >= rj
    tril = jnp.where(causal, 1.0, 0.0).astype(BF16)
    acs16 = _dot3_left(tril, da16 * LOG2E)
    acs_t = acs16.T
    e64 = e64_ref[...]
    dt_full = _dot3(dt16, e64)
    acs_full = _dot3(acs16, e64)
    acs_b = _dot3(acs16, e128_ref[...])
    tot_full = acs_full[l - 1:l, :]
    dte = jnp.exp2(tot_full - acs_full)
    eacs = jnp.exp2(acs_full)
    dec = jnp.exp2(tot_full)
    xdt = xs * dt_full
    xdte = (xdt * dte).astype(BF16)
    glane = lax.broadcasted_iota(jnp.int32, (l, gw), 1) // SSD_HEAD
    head_b = [jnp.where(glane == jj, 1.0, 0.0).astype(BF16) for jj in range(hpg)]

    outs = []
    st_new = []
    for g in range(SSD_GROUPS):
        gs = slice(g * gw, (g + 1) * gw)
        c_g = cm[:, g * SSD_STATE:(g + 1) * SSD_STATE].astype(BF16)
        b_g = bm[:, g * SSD_STATE:(g + 1) * SSD_STATE].astype(BF16)
        scores = _dot_nt(c_g, b_g)
        xdt_g = xdt[:, gs].astype(BF16)
        p_parts = []
        x_parts = []
        for jj in range(hpg):
            j = g * hpg + jj
            diff = acs_b[:, j * LANES:(j + 1) * LANES] - acs_t[j:j + 1, :]
            seg = jnp.exp2(jnp.where(causal, diff, -jnp.inf))
            p_parts.append((scores * seg).astype(BF16))
            x_parts.append(xdt_g * head_b[jj])
        y_diag = _dot(jnp.concatenate(p_parts, axis=1), jnp.concatenate(x_parts, axis=0))
        st = st_ref[g]
        y_off = _dot(c_g, st.astype(BF16)) * eacs[:, gs]
        st_new.append(st * dec[:, gs] + _dot_tn(b_g, xdte[:, gs]))
        y = (y_diag + y_off + d_ref[:, gs] * xs[:, gs]) * zgate[:, gs]
        ms = jnp.mean(y * y, axis=-1, keepdims=True)
        outs.append((y * lax.rsqrt(ms + SSD_NORM_EPS) * nw_ref[:, gs]).astype(o_ref.dtype))
    for g in range(SSD_GROUPS):
        st_ref[g] = st_new[g]
    for g in range(SSD_GROUPS):
        o_ref[0, :, g * gw:(g + 1) * gw] = outs[g]


def _ssd_head_expanders():
    head_of_lane64 = jnp.arange(BRANCH_W) // SSD_HEAD
    head_of_lane128 = jnp.arange(SSD_HEADS * LANES) // LANES
    e64 = (jnp.arange(LANES)[:, None] == head_of_lane64[None, :]).astype(BF16)
    e128 = (jnp.arange(LANES)[:, None] == head_of_lane128[None, :]).astype(BF16)
    return e64, e128


def _ssd_branch(pb, layer, conv_w, conv_b, dt_bias, a_log, d_skip, norm_w, e64, e128):
    bsz, seq, cols = pb.shape
    pad16 = lambda t: jnp.pad(t, ((0, 0), (0, LANES - SSD_HEADS)))
    d_full = jnp.repeat(d_skip, SSD_HEAD, axis=1)
    const = lambda shape: pl.BlockSpec(shape, lambda b, c: (0, 0))
    blocks = (SSD_CHUNK * cols * 4 + (SSD_CONV + 1) * SSD_XBC * 4 + 4 * BRANCH_W * 4
              + e64.size * 2 + e128.size * 2 + SSD_CHUNK * BRANCH_W * 2)
    scratch = (HALO + SSD_CHUNK) * SSD_XBC * 4 + SSD_STATE * BRANCH_W * 4
    return pl.pallas_call(
        _ssd_kernel,
        grid=(bsz, seq // SSD_CHUNK),
        in_specs=[pl.BlockSpec((1, SSD_CHUNK, cols), lambda b, c: (b, c, 0)),
                  _layer_spec((SSD_CONV, SSD_XBC), layer), _layer_spec((1, SSD_XBC), layer),
                  _layer_spec((1, LANES), layer), _layer_spec((1, LANES), layer),
                  _layer_spec((1, BRANCH_W), layer), _layer_spec((1, BRANCH_W), layer),
                  const((LANES, BRANCH_W)), const((LANES, SSD_HEADS * LANES))],
        out_specs=pl.BlockSpec((1, SSD_CHUNK, BRANCH_W), lambda b, c: (b, c, 0)),
        out_shape=jax.ShapeDtypeStruct((bsz, seq, BRANCH_W), BF16),
        scratch_shapes=[pltpu.VMEM((HALO + SSD_CHUNK, SSD_XBC), F32),
                        pltpu.VMEM((SSD_GROUPS, SSD_STATE, BRANCH_W // SSD_GROUPS), F32)],
        compiler_params=_cparams(("parallel", "arbitrary"), blocks, scratch),
    )(pb, conv_w, _rows(conv_b), _rows(pad16(dt_bias)), _rows(pad16(a_log)), _rows(d_full),
      _rows(norm_w), e64, e128)


def _merge_out_kernel(oa_ref, ob_ref, pc_ref, cw_ref, ga_ref, gb_ref, gc_ref, wb_ref, wo_ref,
                      x_ref, nw_ref, *refs, tiles_per_seq):
    o_refs, ext_ref = refs[:-1], refs[-1]
    tm = pc_ref.shape[0]

    @pl.when(lax.rem(pl.program_id(0), tiles_per_seq) == 0)
    def _():
        ext_ref[0:HALO, :] = jnp.zeros((HALO, BRANCH_W), F32)

    u = (pc_ref[:, BRANCH_W:2 * BRANCH_W].astype(F32)
         * pc_ref[:, 2 * BRANCH_W:3 * BRANCH_W].astype(F32))
    ext_ref[HALO:HALO + tm, :] = u
    conv = cw_ref[SC_CONV - 1:SC_CONV, :] * u
    for kk in range(SC_CONV - 1):
        back = SC_CONV - 1 - kk
        conv = conv + cw_ref[kk:kk + 1, :] * ext_ref[pl.ds(HALO - back, tm), :]
    ext_ref[0:HALO, :] = u[tm - HALO:tm, :]
    oc = (pc_ref[:, 0:BRANCH_W].astype(F32) * conv
          * _silu(pc_ref[:, 3 * BRANCH_W:4 * BRANCH_W].astype(F32))).astype(BF16)

    acc = _sigmoid(ga_ref[...].astype(F32)) * _dot(oa_ref[...], wb_ref[0])
    acc = acc + _sigmoid(gb_ref[...].astype(F32)) * _dot(ob_ref[...], wb_ref[1])
    acc = acc + _sigmoid(gc_ref[...].astype(F32)) * _dot(oc, wb_ref[2])
    xn = x_ref[...] + _dot(acc.astype(BF16), wo_ref[...])
    ms = jnp.mean(xn * xn, axis=-1, keepdims=True)
    h_ref = o_refs[-1]
    h_ref[...] = (xn * lax.rsqrt(ms + NORM_EPS) * nw_ref[...]).astype(h_ref.dtype)
    if len(o_refs) == 2:
        o_refs[0][...] = xn


def _merge_out(oa, ob, pcg, layer, conv_w, wb, wo, x, nw, seq, norm_dtype, keep_residual, tm=256):
    m = oa.shape[0]
    d = D_MODEL
    assert seq % tm == 0
    row = lambda i: (i, 0)
    once = pl.Buffered(1)
    o_spec = pl.BlockSpec((tm, BRANCH_W), row)
    g_spec = lambda nb: pl.BlockSpec((tm, d), lambda i: (i, SC_COLS // d + nb))
    norm_bytes = jnp.dtype(norm_dtype).itemsize
    blocks = (2 * tm * BRANCH_W * 2 + tm * SC_COLS * pcg.dtype.itemsize + SC_CONV * BRANCH_W * 4
              + N_BRANCH * tm * d * pcg.dtype.itemsize + tm * d * (4 + norm_bytes) + d * 4)
    out_specs = [pl.BlockSpec((tm, d), row)]
    out_shape = [jax.ShapeDtypeStruct((m, d), norm_dtype)]
    if keep_residual:
        blocks += tm * d * 4
        out_specs.insert(0, pl.BlockSpec((tm, d), row))
        out_shape.insert(0, jax.ShapeDtypeStruct((m, d), F32))
    resident = (wb.size * 2 + wo.size * 2 + tm * d * 6
                + (HALO + tm) * BRANCH_W * 4)
    return pl.pallas_call(
        functools.partial(_merge_out_kernel, tiles_per_seq=seq // tm),
        grid=(m // tm,),
        in_specs=[o_spec, o_spec, pl.BlockSpec((tm, SC_COLS), row),
                  _layer_spec((SC_CONV, BRANCH_W), layer), g_spec(0), g_spec(1), g_spec(2),
                  pl.BlockSpec((N_BRANCH, BRANCH_W, d), lambda i: (0, 0, 0), pipeline_mode=once),
                  pl.BlockSpec((d, d), lambda i: (0, 0), pipeline_mode=once),
                  pl.BlockSpec((tm, d), row),
                  pl.BlockSpec((1, d), lambda i: (0, 0))],
        out_specs=out_specs,
        out_shape=out_shape,
        scratch_shapes=[pltpu.VMEM((HALO + tm, BRANCH_W), F32)],
        compiler_params=_cparams(("arbitrary",), blocks, resident),
    )(oa, ob, pcg, conv_w, pcg, pcg, pcg, wb, wo, x, nw.reshape(1, d))


def _layer(x2, h, bsz, seq, next_norm_w, last, layer, p):
    m = bsz * seq
    wt = p["w_in_t"]
    c0, c1 = RW_COLS, RW_COLS + SSD_COLS
    w_a = _cast_rows(wt, layer, 0, RW_COLS, RW_COLS, tr=1056)
    w_b = _cast_rows(wt, layer, c0, SSD_COLS, SSD_COLS_PAD, tr=640)
    w_cg = _cast_rows(wt, layer, c1, SC_COLS + GATE_COLS, SC_COLS + GATE_COLS, tr=1024)

    pa = _matmul_shift(h, w_a, p["rw_mu"], layer, seq, tm=512, tn=RW_COLS)
    pb = _matmul(h, w_b, F32, tm=512, tn=SSD_COLS_PAD)
    pcg = _matmul(h, w_cg, BF16, tm=1024, tn=2048)

    oa = _rwkv_branch(pa.reshape(bsz, seq, RW_COLS), layer, p["rw_w0"], p["rw_w2a2"], p["rw_a0"],
                      p["rw_kk"], p["rw_ka"], p["rw_rk"], p["rw_ln_w"], p["rw_ln_b"])
    ob = _ssd_branch(pb.reshape(bsz, seq, SSD_COLS_PAD), layer, p["ssd_conv_w"], p["ssd_conv_b"],
                     p["ssd_dt_bias"], p["ssd_a_log"], p["ssd_d"], p["ssd_norm_w"],
                     p["ssd_e64"], p["ssd_e128"])

    wb = _cast_bf16(p["w_branch"].reshape(-1, N_BRANCH * BRANCH_W, D_MODEL), layer)
    outs = _merge_out(oa.reshape(m, BRANCH_W), ob.reshape(m, BRANCH_W), pcg, layer,
                      p["sc_conv_w"], wb.reshape(N_BRANCH, BRANCH_W, D_MODEL),
                      _cast_bf16(p["w_out"], layer), x2, next_norm_w, seq,
                      F32 if last else BF16, keep_residual=not last)
    if last:
        return None, outs[0]
    return outs


def kernel(x, norm_w, w_in, rw_mu, rw_w0, rw_w2, rw_a0, rw_a2, rw_kk, rw_ka, rw_rk, rw_ln_w,
           rw_ln_b, ssd_conv_w, ssd_conv_b, ssd_dt_bias, ssd_a_log, ssd_d, ssd_norm_w,
           sc_conv_w, w_branch, w_out, final_norm_w):
    bsz, seq, d = x.shape
    depth = w_in.shape[0]
    zeros = jnp.zeros((depth, RW_RANK, BRANCH_W), F32)
    rw_w2a2 = jnp.concatenate([jnp.concatenate([rw_w2, zeros], axis=2),
                               jnp.concatenate([zeros, rw_a2], axis=2)], axis=1).astype(BF16)
    e64, e128 = _ssd_head_expanders()
    p = dict(w_in_t=jnp.transpose(w_in, (0, 2, 1)), w_branch=w_branch, w_out=w_out, rw_mu=rw_mu,
             rw_w0=rw_w0, rw_w2a2=rw_w2a2,
             rw_a0=rw_a0, rw_kk=rw_kk, rw_ka=rw_ka, rw_rk=rw_rk, rw_ln_w=rw_ln_w, rw_ln_b=rw_ln_b,
             ssd_conv_w=ssd_conv_w, ssd_conv_b=ssd_conv_b, ssd_dt_bias=ssd_dt_bias,
             ssd_a_log=ssd_a_log, ssd_d=ssd_d, ssd_norm_w=ssd_norm_w, ssd_e64=e64, ssd_e128=e128,
             sc_conv_w=sc_conv_w)
    x2 = x.reshape(bsz * seq, d)
    h = _rmsnorm(x2, norm_w[0], BF16)
    for l in range(depth):
        last = l == depth - 1
        next_norm_w = final_norm_w if last else norm_w[l + 1]
        x2, h = _layer(x2, h, bsz, seq, next_norm_w, last, l, p)
    return h.reshape(bsz, seq, d)
```
